```python
import math
import jax
import jax.numpy as jnp
from jax import lax
import numpy as np


D_MODEL = 2048
BATCH = 4
SEQ = 4096
DEPTH = 4

HEAD_DIM = 128
N_HEADS = D_MODEL // HEAD_DIM
A_HEADS = N_HEADS // 2
A_SUB = HEAD_DIM // 2
B_HEADS = N_HEADS - A_HEADS
A_WIDTH = A_HEADS * HEAD_DIM
B_WIDTH = B_HEADS * HEAD_DIM
AB_IN = 3 * A_WIDTH + 4 * B_WIDTH + 2 * B_HEADS
C_HEADS = N_HEADS
C_BRANCHES = ((128, 1), (512, 4), (2048, 16))
D_FF = 4 * D_MODEL
CONV_K = 4
CHUNK = 64
Q_BLOCK = 128
ROPE_THETA = 10000.0
NORM_EPS = 1e-6
N_EVEN = (DEPTH + 1) // 2
N_ODD = DEPTH // 2

kernel_name = 'hybrid_diffattn_gdn_dilated_trunk'


def rms_norm(x, g):
    xf = x.astype(jnp.float32)
    y = xf * lax.rsqrt(jnp.mean(xf * xf, axis=-1, keepdims=True) + NORM_EPS)
    return (y * g.astype(jnp.float32)).astype(x.dtype)


def l2_norm(x):
    xf = x.astype(jnp.float32)
    return xf * lax.rsqrt(jnp.sum(xf * xf, axis=-1, keepdims=True) + NORM_EPS)


def rope_tables(seq, dim):
    inv = 1.0 / (ROPE_THETA ** (jnp.arange(0, dim, 2, dtype=jnp.float32) / dim))
    ang = jnp.arange(seq, dtype=jnp.float32)[:, None] * inv[None, :]
    return jnp.cos(ang), jnp.sin(ang)


def apply_rope(x, cos, sin):
    x1, x2 = jnp.split(x.astype(jnp.float32), 2, axis=-1)
    c = cos[None, :, None, :]
    s = sin[None, :, None, :]
    return jnp.concatenate([x1 * c - x2 * s, x2 * c + x1 * s], axis=-1).astype(x.dtype)


def causal_depthwise_conv(x, w):
    k, c = w.shape
    return lax.conv_general_dilated(x, w[:, None, :].astype(x.dtype), window_strides=(1,),
                                    padding=((k - 1, 0),), dimension_numbers=('NWC', 'WIO', 'NWC'),
                                    feature_group_count=c)


def diff_attention(q, k, v, lam):
    bsz, seq = q.shape[:2]
    nblk = seq // Q_BLOCK
    scale = A_SUB ** -0.5
    kf = k.astype(jnp.float32)
    vf = v.astype(jnp.float32)
    qb = jnp.moveaxis(q.reshape(bsz, nblk, Q_BLOCK, 2 * A_HEADS, A_SUB), 1, 0)
    kpos = jnp.arange(seq)

    def one_block(args):
        qi, i = args
        s = jnp.einsum('bqhd,bkhd->bhqk', qi.astype(jnp.float32), kf) * scale
        qpos = i * Q_BLOCK + jnp.arange(Q_BLOCK)
        s = jnp.where(kpos[None, :] <= qpos[:, None], s, -jnp.inf)
        p = jax.nn.softmax(s, axis=-1).reshape(bsz, A_HEADS, 2, Q_BLOCK, seq)
        a = p[:, :, 0] - lam * p[:, :, 1]
        return jnp.einsum('bhqk,bkhd->bqhd', a, vf)

    o = lax.map(one_block, (qb, jnp.arange(nblk)))
    return jnp.moveaxis(o, 0, 1).reshape(bsz, seq, A_HEADS, HEAD_DIM)


def gated_delta_rule(q, k, v, g, beta):
    bsz, seq, nh, dk = q.shape
    dv = v.shape[-1]
    n = seq // CHUNK
    f32 = jnp.float32

    def chunks(t):
        t = t.astype(f32).reshape((bsz, n, CHUNK, nh) + t.shape[3:])
        return jnp.moveaxis(t, 3, 1)

    q = chunks(q) * dk ** -0.5
    k = chunks(k)
    v = chunks(v)
    g = chunks(g)
    beta = chunks(beta)
    gc = jnp.cumsum(g, axis=-1)
    idx = jnp.arange(CHUNK)
    causal = idx[:, None] >= idx[None, :]
    strict = idx[:, None] > idx[None, :]
    decay_incl = jnp.exp(jnp.where(causal, gc[..., :, None] - gc[..., None, :], -jnp.inf))
    decay_strict = jnp.where(strict, decay_incl, 0.0)
    kb = k * beta[..., None]
    m = jnp.einsum('bhncd,bhnjd->bhncj', kb, k) * decay_strict
    eye = jnp.eye(CHUNK, dtype=f32)
    rhs = jnp.concatenate([v * beta[..., None], kb * jnp.exp(gc)[..., None]], axis=-1)
    sol = lax.linalg.triangular_solve(eye + m, rhs, left_side=True, lower=True, unit_diagonal=True)
    u, w = sol[..., :dv], sol[..., dv:]
    attn = jnp.einsum('bhncd,bhnjd->bhncj', q, k) * decay_incl
    q_dec = q * jnp.exp(gc)[..., None]
    g_last = gc[..., -1]
    k_dec = k * jnp.exp(g_last[..., None] - gc)[..., None]

    def step(state, xs):
        u_i, w_i, attn_i, qd_i, kd_i, gl_i = xs
        v_new = u_i - jnp.einsum('bhcd,bhde->bhce', w_i, state)
        o_i = jnp.einsum('bhcd,bhde->bhce', qd_i, state) + jnp.einsum('bhcj,bhje->bhce', attn_i, v_new)
        state = state * jnp.exp(gl_i)[..., None, None] + jnp.einsum('bhcd,bhce->bhde', kd_i, v_new)
        return state, o_i

    xs = tuple(jnp.moveaxis(t, 2, 0) for t in (u, w, attn, q_dec, k_dec, g_last))
    state0 = jnp.zeros((bsz, nh, dk, dv), f32)
    _, o = lax.scan(step, state0, xs)
    return o.transpose(1, 0, 3, 2, 4).reshape(bsz, seq, nh, dv)


def diff_delta_mixer(h, layer_idx, w_in, a_q_norm, a_k_norm, a_lambda, a_sub_norm,
                     b_conv, b_a_log, b_dt_bias, b_out_norm, w_out, cos_a, sin_a):
    bsz, seq, _ = h.shape
    f32 = jnp.float32
    proj = h @ w_in
    cuts = [A_WIDTH, 2 * A_WIDTH, 3 * A_WIDTH, 3 * A_WIDTH + 3 * B_WIDTH,
            3 * A_WIDTH + 4 * B_WIDTH, 3 * A_WIDTH + 4 * B_WIDTH + B_HEADS]
    aq, ak, av, bqkv, bz, ba, bb = jnp.split(proj, cuts, axis=-1)
    aq = apply_rope(rms_norm(aq.reshape(bsz, seq, 2 * A_HEADS, A_SUB), a_q_norm), cos_a, sin_a)
    ak = apply_rope(rms_norm(ak.reshape(bsz, seq, 2 * A_HEADS, A_SUB), a_k_norm), cos_a, sin_a)
    av = av.reshape(bsz, seq, A_HEADS, HEAD_DIM)
    lam_init = 0.8 - 0.6 * math.exp(-0.3 * layer_idx)
    lv = a_lambda.astype(f32)
    lam = jnp.exp(jnp.sum(lv[0] * lv[1])) - jnp.exp(jnp.sum(lv[2] * lv[3])) + lam_init
    oa = diff_attention(aq, ak, av, lam)
    oa = (rms_norm(oa, a_sub_norm) * (1.0 - lam_init)).astype(h.dtype)
    bqkv = jax.nn.silu(causal_depthwise_conv(bqkv, b_conv))
    bq, bk, bv = jnp.split(bqkv, 3, axis=-1)
    bq = l2_norm(bq.reshape(bsz, seq, B_HEADS, HEAD_DIM))
    bk = l2_norm(bk.reshape(bsz, seq, B_HEADS, HEAD_DIM))
    bv = bv.reshape(bsz, seq, B_HEADS, HEAD_DIM)
    g = -jnp.exp(b_a_log.astype(f32)) * jax.nn.softplus((ba + b_dt_bias).astype(f32))
    beta = jax.nn.sigmoid(bb.astype(f32))
    ob = gated_delta_rule(bq, bk, bv, g, beta)
    ob = rms_norm(ob, b_out_norm) * jax.nn.silu(bz.reshape(bsz, seq, B_HEADS, HEAD_DIM).astype(f32))
    ob = ob.astype(h.dtype)
    o = jnp.concatenate([oa.reshape(bsz, seq, A_WIDTH), ob.reshape(bsz, seq, B_WIDTH)], axis=-1)
    return o @ w_out


def dilated_branch(q, k, v, window, dil):
    bsz, seq, nh, hd = q.shape
    hops = window // dil
    s_pad = -(-seq // (dil * hops)) * (dil * hops)
    sub_len = s_pad // dil
    n = sub_len // hops
    pad = ((0, 0), (0, s_pad - seq), (0, 0), (0, 0))

    def strided(t):
        t = jnp.pad(t.astype(jnp.float32), pad).reshape(bsz, sub_len, dil, nh, hd)
        return t.transpose(0, 2, 3, 1, 4).reshape(bsz, dil, nh, n, hops, hd)

    def with_prev(t):
        prev = jnp.pad(t, ((0, 0), (0, 0), (0, 0), (1, 0), (0, 0), (0, 0)))[:, :, :, :-1]
        return jnp.concatenate([prev, t], axis=4)

    qs = strided(q)
    kb = with_prev(strided(k))
    vb = with_prev(strided(v))
    s = jnp.einsum('brhnqd,brhnkd->brhnqk', qs, kb) * hd ** -0.5
    a = jnp.arange(hops)[:, None]
    b = jnp.arange(2 * hops)[None, :]
    dist = a + hops - b
    band = (dist >= 0) & (dist <= hops)
    first = (jnp.arange(n) == 0)[:, None, None]
    mask = band[None] & ~(first & (b < hops)[None])
    s = jnp.where(mask, s, -jnp.inf)
    mx = jnp.max(s, axis=-1, keepdims=True)
    p = jnp.exp(s - mx)
    den = jnp.sum(p, axis=-1, keepdims=True)
    o = jnp.einsum('brhnqk,brhnkd->brhnqd', p, vb) / den
    lse = (mx + jnp.log(den))[..., 0]
    o = o.reshape(bsz, dil, nh, sub_len, hd).transpose(0, 3, 1, 2, 4).reshape(bsz, s_pad, nh, hd)
    lse = lse.reshape(bsz, dil, nh, sub_len).transpose(0, 3, 1, 2).reshape(bsz, s_pad, nh)
    return o[:, :seq], lse[:, :seq]


def dilated_mixer(h, w_in, q_norm, k_norm, w_out, cos_c, sin_c):
    bsz, seq, _ = h.shape
    q, k, v = jnp.split(h @ w_in, 3, axis=-1)
    q = apply_rope(rms_norm(q.reshape(bsz, seq, C_HEADS, HEAD_DIM), q_norm), cos_c, sin_c)
    k = apply_rope(rms_norm(k.reshape(bsz, seq, C_HEADS, HEAD_DIM), k_norm), cos_c, sin_c)
    v = v.reshape(bsz, seq, C_HEADS, HEAD_DIM)
    outs = []
    lses = []
    for window, dil in C_BRANCHES:
        o_g, lse_g = dilated_branch(q, k, v, window, dil)
        outs.append(o_g)
        lses.append(lse_g)
    wts = jax.nn.softmax(jnp.stack(lses, axis=0), axis=0)
    o = jnp.einsum('gbsh,gbshd->bshd', wts, jnp.stack(outs, axis=0))
    return o.reshape(bsz, seq, C_HEADS * HEAD_DIM).astype(h.dtype) @ w_out


def squared_relu_mlp(h, w1, w2):
    return jnp.square(jax.nn.relu(h @ w1)) @ w2


def setup_inputs(seed: int = 0) -> dict:
    key = jax.random.key(seed)
    ks = jax.random.split(key, 20)
    f32 = jnp.float32

    def nrm(k, shape, scale):
        return jax.random.normal(k, shape, f32) * scale

    def gain(k, shape):
        return 1.0 + 0.02 * jax.random.normal(k, shape, f32)

    dt = jnp.exp(jax.random.uniform(ks[9], (N_EVEN, B_HEADS), f32, math.log(1e-3), math.log(1e-1)))
    return {
        'x': nrm(ks[0], (BATCH, SEQ, D_MODEL), 1.0),
        'ab_norm': gain(ks[1], (N_EVEN, D_MODEL)),
        'ab_w_in': nrm(ks[2], (N_EVEN, D_MODEL, AB_IN), D_MODEL ** -0.5),
        'a_q_norm': gain(ks[3], (N_EVEN, A_SUB)),
        'a_k_norm': gain(ks[4], (N_EVEN, A_SUB)),
        'a_lambda': nrm(ks[5], (N_EVEN, 4, A_SUB), 0.1),
        'a_sub_norm': gain(ks[6], (N_EVEN, HEAD_DIM)),
        'b_conv': nrm(ks[7], (N_EVEN, CONV_K, 3 * B_WIDTH), CONV_K ** -0.5),
        'b_a_log': jnp.log(jax.random.uniform(ks[8], (N_EVEN, B_HEADS), f32, 1.0, 16.0)),
        'b_dt_bias': dt + jnp.log(-jnp.expm1(-dt)),
        'b_out_norm': gain(ks[10], (N_EVEN, HEAD_DIM)),
        'ab_w_out': nrm(ks[11], (N_EVEN, D_MODEL, D_MODEL), D_MODEL ** -0.5),
        'c_norm': gain(ks[12], (N_ODD, D_MODEL)),
        'c_w_in': nrm(ks[13], (N_ODD, D_MODEL, 3 * C_HEADS * HEAD_DIM), D_MODEL ** -0.5),
        'c_q_norm': gain(ks[14], (N_ODD, HEAD_DIM)),
        'c_k_norm': gain(ks[15], (N_ODD, HEAD_DIM)),
        'c_w_out': nrm(ks[16], (N_ODD, C_HEADS * HEAD_DIM, D_MODEL), D_MODEL ** -0.5),
        'mlp_norm': gain(ks[17], (DEPTH, D_MODEL)),
        'mlp_w1': nrm(ks[18], (DEPTH, D_MODEL, D_FF), D_MODEL ** -0.5),
        'mlp_w2': nrm(ks[19], (DEPTH, D_FF, D_MODEL), D_FF ** -0.5),
    }


def reference(x, ab_norm, ab_w_in, a_q_norm, a_k_norm, a_lambda, a_sub_norm, b_conv, b_a_log,
              b_dt_bias, b_out_norm, ab_w_out, c_norm, c_w_in, c_q_norm, c_k_norm, c_w_out,
              mlp_norm, mlp_w1, mlp_w2):
    seq = x.shape[1]
    cos_a, sin_a = rope_tables(seq, A_SUB)
    cos_c, sin_c = rope_tables(seq, HEAD_DIM)
    for l in range(DEPTH):
        i = l // 2
        if l % 2 == 0:
            x = x + diff_delta_mixer(rms_norm(x, ab_norm[i]), l, ab_w_in[i], a_q_norm[i], a_k_norm[i],
                                     a_lambda[i], a_sub_norm[i], b_conv[i], b_a_log[i], b_dt_bias[i],
                                     b_out_norm[i], ab_w_out[i], cos_a, sin_a)
        else:
            x = x + dilated_mixer(rms_norm(x, c_norm[i]), c_w_in[i], c_q_norm[i], c_k_norm[i],
                                  c_w_out[i], cos_c, sin_c)
        x = x + squared_relu_mlp(rms_norm(x, mlp_norm[l]), mlp_w1[l], mlp_w2[l])
    return x
```

```python
import functools
import math

import jax
import jax.numpy as jnp
from jax import lax
from jax.experimental import pallas as pl
from jax.experimental.pallas import tpu as pltpu

F32 = jnp.float32
BF16 = jnp.bfloat16

HEAD_DIM = 128
A_SUB = HEAD_DIM // 2
CONV_K = 4
CHUNK = 64
ROPE_THETA = 10000.0
NORM_EPS = 1e-6
C_BRANCHES = ((128, 1), (512, 4), (2048, 16))
LANES = 128
VMEM_LIMIT_BYTES = 56 * 1024 * 1024


def _params(*semantics):
    return pltpu.CompilerParams(dimension_semantics=semantics, vmem_limit_bytes=VMEM_LIMIT_BYTES)


def _rms_rows(x, gain):
    ms = jnp.mean(x * x, axis=-1, keepdims=True)
    return x * lax.rsqrt(ms + NORM_EPS) * gain


def _dot(a, b):
    return jnp.dot(a, b, preferred_element_type=F32)


def _dot_nt(a, b):
    return lax.dot_general(a, b, (((1,), (1,)), ((), ())), preferred_element_type=F32)


def _dot_tn(a, b):
    return lax.dot_general(a, b, (((0,), (0,)), ((), ())), preferred_element_type=F32)


def _split_bf16(x):
    hi = x.astype(BF16)
    lo = (x - hi.astype(F32)).astype(BF16)
    return hi, lo


def _norm_matmul_kernel(x_ref, g_ref, w_ref, o_ref, xn_ref):
    @pl.when(pl.program_id(1) == 0)
    def _():
        xn_ref[...] = _rms_rows(x_ref[...], g_ref[...]).astype(BF16)

    o_ref[...] = _dot(xn_ref[...], w_ref[...]).astype(o_ref.dtype)


def _norm_matmul(x, g, w, out_dtype, tm, tn):
    t, d = x.shape
    n = w.shape[1]
    tn = min(tn, n)
    return pl.pallas_call(
        _norm_matmul_kernel,
        grid=(t // tm, n // tn),
        in_specs=[
            pl.BlockSpec((tm, d), lambda i, j: (i, 0)),
            pl.BlockSpec((1, d), lambda i, j: (0, 0)),
            pl.BlockSpec((d, tn), lambda i, j: (0, j)),
        ],
        out_specs=pl.BlockSpec((tm, tn), lambda i, j: (i, j)),
        out_shape=jax.ShapeDtypeStruct((t, n), out_dtype),
        scratch_shapes=[pltpu.VMEM((tm, d), BF16)],
        compiler_params=_params("parallel", "arbitrary"),
        name="norm_matmul",
    )(x, g, w)


def _out_proj_kernel(*refs, n_in):
    x_ref = refs[0]
    o_refs = refs[1:1 + n_in]
    w_refs = refs[1 + n_in:1 + 2 * n_in]
    out_ref = refs[1 + 2 * n_in]
    acc = x_ref[...]
    for o_ref, w_ref in zip(o_refs, w_refs):
        acc = acc + _dot(o_ref[...], w_ref[...])
    out_ref[...] = acc


def _out_proj(x, os_, ws, tm, tn):
    t, d = x.shape
    n_in = len(os_)
    in_specs = [pl.BlockSpec((tm, tn), lambda i, j: (i, j))]
    in_specs += [pl.BlockSpec((tm, o.shape[1]), lambda i, j: (i, 0)) for o in os_]
    in_specs += [pl.BlockSpec((w.shape[0], tn), lambda i, j: (0, j)) for w in ws]
    return pl.pallas_call(
        functools.partial(_out_proj_kernel, n_in=n_in),
        grid=(t // tm, d // tn),
        in_specs=in_specs,
        out_specs=pl.BlockSpec((tm, tn), lambda i, j: (i, j)),
        out_shape=jax.ShapeDtypeStruct((t, d), F32),
        compiler_params=_params("parallel", "arbitrary"),
        name="out_proj",
    )(x, *os_, *ws)


def _mlp_kernel(x_ref, g_ref, w1_ref, w2_ref, o_ref, xn_ref):
    @pl.when(pl.program_id(1) == 0)
    def _():
        x = x_ref[...]
        xn_ref[...] = _rms_rows(x, g_ref[...]).astype(BF16)
        o_ref[...] = x

    h = jnp.maximum(_dot(xn_ref[...], w1_ref[...]), 0.0)
    o_ref[...] += _dot((h * h).astype(BF16), w2_ref[...])


def _mlp(x, g, w1, w2, tm, tf):
    t, d = x.shape
    f = w1.shape[1]
    return pl.pallas_call(
        _mlp_kernel,
        grid=(t // tm, f // tf),
        in_specs=[
            pl.BlockSpec((tm, d), lambda i, j: (i, 0)),
            pl.BlockSpec((1, d), lambda i, j: (0, 0)),
            pl.BlockSpec((d, tf), lambda i, j: (0, j)),
            pl.BlockSpec((tf, d), lambda i, j: (j, 0)),
        ],
        out_specs=pl.BlockSpec((tm, d), lambda i, j: (i, 0)),
        out_shape=jax.ShapeDtypeStruct((t, d), F32),
        scratch_shapes=[pltpu.VMEM((tm, d), BF16)],
        compiler_params=_params("parallel", "arbitrary"),
        name="mlp",
    )(x, g, w1, w2)


def _rope_tables(seq, dim):
    half = dim // 2
    inv = 1.0 / (ROPE_THETA ** (jnp.arange(0, dim, 2, dtype=F32) / dim))
    ang = jnp.arange(seq, dtype=F32)[:, None] * inv[None, :]
    cos, sin = jnp.cos(ang), jnp.sin(ang)
    reps = LANES // dim
    cos_t = jnp.tile(jnp.concatenate([cos, cos], axis=-1), (1, reps))
    sin_t = jnp.tile(jnp.concatenate([-sin, sin], axis=-1), (1, reps))
    return cos_t, sin_t


def _rotate_half(y, dim):
    half = dim // 2
    if dim == LANES:
        return pltpu.roll(y, half, axis=1)
    lane = lax.broadcasted_iota(jnp.int32, y.shape, 1)
    first = (lane % dim) < half
    return jnp.where(first, pltpu.roll(y, LANES - half, axis=1), pltpu.roll(y, half, axis=1))


def _qk64_kernel(p_ref, g_ref, cos_ref, sin_ref, grp_ref, o_ref):
    cos = cos_ref[...]
    sin = sin_ref[...]
    grp = grp_ref[...]
    for t in range(p_ref.shape[2] // LANES):
        cols = slice(t * LANES, (t + 1) * LANES)
        x = p_ref[0, :, cols].astype(F32)
        hi, lo = _split_bf16(x * x)
        ms = (_dot(hi, grp) + _dot(lo, grp)) * (1.0 / A_SUB)
        y = x * lax.rsqrt(ms + NORM_EPS) * g_ref[:, cols]
        o_ref[0, :, cols] = (y * cos + _rotate_half(y, A_SUB) * sin).astype(o_ref.dtype)


def _qk64_prologue(proj, gain, cos_t, sin_t, width, ts, cw):
    b, s, _ = proj.shape
    lane = jnp.arange(LANES)
    grp = (lane[:, None] // A_SUB == lane[None, :] // A_SUB).astype(BF16)
    return pl.pallas_call(
        _qk64_kernel,
        grid=(b, s // ts, width // cw),
        in_specs=[
            pl.BlockSpec((1, ts, cw), lambda bi, i, j: (bi, i, j)),
            pl.BlockSpec((1, cw), lambda bi, i, j: (0, j)),
            pl.BlockSpec((ts, LANES), lambda bi, i, j: (i, 0)),
            pl.BlockSpec((ts, LANES), lambda bi, i, j: (i, 0)),
            pl.BlockSpec((LANES, LANES), lambda bi, i, j: (0, 0)),
        ],
        out_specs=pl.BlockSpec((1, ts, cw), lambda bi, i, j: (bi, i, j)),
        out_shape=jax.ShapeDtypeStruct((b, s, width), BF16),
        compiler_params=_params("parallel", "parallel", "arbitrary"),
        name="qk64_prologue",
    )(proj, gain, cos_t, sin_t, grp)


def _diff_attn_kernel(q_ref, k_ref, v_ref, lam_ref, gn_ref, o_ref, q2_ref, m_ref, l_ref, acc_ref,
                      *, tq, tk, lam_init):
    i = pl.program_id(2)
    j = pl.program_id(3)

    @pl.when(j == 0)
    def _():
        q = q_ref[0]
        lane = lax.broadcasted_iota(jnp.int32, q.shape, 1)
        zero = jnp.zeros_like(q)
        q2_ref[:tq, :] = jnp.where(lane < A_SUB, q, zero)
        q2_ref[tq:, :] = jnp.where(lane < A_SUB, zero, q)
        m_ref[...] = jnp.full(m_ref.shape, -jnp.inf, F32)
        l_ref[...] = jnp.zeros(l_ref.shape, F32)
        acc_ref[...] = jnp.zeros(acc_ref.shape, F32)

    def step(masked):
        s = _dot_nt(q2_ref[...], k_ref[0])
        if masked:
            row = lax.broadcasted_iota(jnp.int32, s.shape, 0)
            col = lax.broadcasted_iota(jnp.int32, s.shape, 1)
            qpos = i * tq + jnp.where(row >= tq, row - tq, row)
            s = jnp.where(j * tk + col <= qpos, s, -jnp.inf)
        m_prev = m_ref[...]
        m_new = jnp.maximum(m_prev, jnp.max(s, axis=1, keepdims=True))
        alpha = jnp.exp(m_prev - m_new)
        p = jnp.exp(s - m_new)
        l_ref[...] = alpha * l_ref[...] + jnp.sum(p, axis=1, keepdims=True)
        acc_ref[...] = alpha * acc_ref[...] + _dot(p.astype(BF16), v_ref[0])
        m_ref[...] = m_new

    first_q = i * tq
    last_q = first_q + tq - 1
    first_k = j * tk
    last_k = first_k + tk - 1

    @pl.when(last_k <= first_q)
    def _():
        step(False)

    @pl.when(jnp.logical_and(last_k > first_q, first_k <= last_q))
    def _():
        step(True)

    @pl.when(j == pl.num_programs(3) - 1)
    def _():
        o12 = acc_ref[...] / l_ref[...]
        lv = lam_ref[...]
        lam = (jnp.exp(jnp.sum(lv[0:1] * lv[1:2], axis=1, keepdims=True))
               - jnp.exp(jnp.sum(lv[2:3] * lv[3:4], axis=1, keepdims=True)) + lam_init)
        o = o12[:tq] - lam * o12[tq:]
        o_ref[0] = _rms_rows(o, gn_ref[...]).astype(o_ref.dtype)


def _diff_attention(qk, proj, v_col0, a_lambda, gn, lam_init, n_heads, tq, tk):
    b, s, _ = qk.shape
    kernel = functools.partial(_diff_attn_kernel, tq=tq, tk=tk, lam_init=lam_init)
    v_blk0 = v_col0 // HEAD_DIM

    def last_k_block(i):
        return ((i + 1) * tq - 1) // tk

    return pl.pallas_call(
        kernel,
        grid=(b, n_heads, s // tq, s // tk),
        in_specs=[
            pl.BlockSpec((1, tq, HEAD_DIM), lambda bi, h, i, j: (bi, i, h)),
            pl.BlockSpec((1, tk, HEAD_DIM),
                         lambda bi, h, i, j: (bi, jnp.minimum(j, last_k_block(i)), n_heads + h)),
            pl.BlockSpec((1, tk, HEAD_DIM),
                         lambda bi, h, i, j: (bi, jnp.minimum(j, last_k_block(i)), v_blk0 + h)),
            pl.BlockSpec((4, A_SUB), lambda bi, h, i, j: (0, 0)),
            pl.BlockSpec((1, HEAD_DIM), lambda bi, h, i, j: (0, 0)),
        ],
        out_specs=pl.BlockSpec((1, tq, HEAD_DIM), lambda bi, h, i, j: (bi, i, h)),
        out_shape=jax.ShapeDtypeStruct((b, s, n_heads * HEAD_DIM), BF16),
        scratch_shapes=[
            pltpu.VMEM((2 * tq, HEAD_DIM), BF16),
            pltpu.VMEM((2 * tq, 1), F32),
            pltpu.VMEM((2 * tq, 1), F32),
            pltpu.VMEM((2 * tq, HEAD_DIM), F32),
        ],
        compiler_params=_params("parallel", "parallel", "parallel", "arbitrary"),
        name="diff_attention",
    )(qk, qk, proj, a_lambda, gn)


def _dilated_kernel(q_ref, k_ref, v_ref, gq_ref, gk_ref, cos_ref, sin_ref, o_ref,
                    qf_ref, kf_ref, vf_ref, ob0, ob1, ob2, ls0, ls1, ls2, *, seq, rows):
    def prep(c, carry):
        sl = pl.ds(pl.multiple_of(c * rows, rows), rows)
        cos = cos_ref[sl, :]
        sin = sin_ref[sl, :]
        for src, g_ref, dst in ((q_ref, gq_ref, qf_ref), (k_ref, gk_ref, kf_ref)):
            y = _rms_rows(src[0, sl, :].astype(F32), g_ref[...])
            dst[sl, :] = y * cos + _rotate_half(y, HEAD_DIM) * sin
        vf_ref[sl, :] = v_ref[0, sl, :].astype(F32)
        return carry

    lax.fori_loop(0, seq // rows, prep, 0)

    for (window, dil), ob_ref, ls_ref in zip(C_BRANCHES, (ob0, ob1, ob2), (ls0, ls1, ls2)):
        hops = window // dil
        span = dil * hops
        n_blk = seq // span
        qi = lax.broadcasted_iota(jnp.int32, (hops, hops), 0)
        kj = lax.broadcasted_iota(jnp.int32, (hops, hops), 1)
        own_mask = kj <= qi
        prev_mask = kj >= qi

        def block(idx, carry, dil=dil, hops=hops, span=span, n_blk=n_blk, ob_ref=ob_ref,
                  ls_ref=ls_ref, own_mask=own_mask, prev_mask=prev_mask):
            r = idx // n_blk
            n = idx - r * n_blk
            base = r + n * span
            own = pl.ds(base, hops, stride=dil)
            prev = pl.ds(jnp.maximum(base - span, r), hops, stride=dil)
            q = qf_ref[own, :].astype(BF16)
            s1 = jnp.where(own_mask, _dot_nt(q, kf_ref[own, :].astype(BF16)), -jnp.inf)
            s0 = jnp.where(jnp.logical_and(prev_mask, n > 0),
                           _dot_nt(q, kf_ref[prev, :].astype(BF16)), -jnp.inf)
            m = jnp.maximum(jnp.max(s1, axis=1, keepdims=True), jnp.max(s0, axis=1, keepdims=True))
            p1 = jnp.exp(s1 - m)
            p0 = jnp.exp(s0 - m)
            den = jnp.sum(p1, axis=1, keepdims=True) + jnp.sum(p0, axis=1, keepdims=True)
            o = _dot(p1.astype(BF16), vf_ref[own, :].astype(BF16))
            o = o + _dot(p0.astype(BF16), vf_ref[prev, :].astype(BF16))
            ob_ref[own, :] = o / den
            ls_ref[own, :] = jnp.broadcast_to(m + jnp.log(den), (hops, HEAD_DIM))
            return carry

        lax.fori_loop(0, dil * n_blk, block, 0)

    def merge(c, carry):
        sl = pl.ds(pl.multiple_of(c * rows, rows), rows)
        l0, l1, l2 = ls0[sl, :], ls1[sl, :], ls2[sl, :]
        m = jnp.maximum(jnp.maximum(l0, l1), l2)
        e0, e1, e2 = jnp.exp(l0 - m), jnp.exp(l1 - m), jnp.exp(l2 - m)
        o = (e0 * ob0[sl, :] + e1 * ob1[sl, :] + e2 * ob2[sl, :]) / (e0 + e1 + e2)
        o_ref[0, sl, :] = o.astype(o_ref.dtype)
        return carry

    lax.fori_loop(0, seq // rows, merge, 0)


def _dilated_attention(proj, gq, gk, cos_t, sin_t, n_heads, rows):
    b, s, _ = proj.shape
    seq_spec = lambda off: pl.BlockSpec((1, s, HEAD_DIM), lambda bi, h: (bi, 0, off + h))
    full = lambda shape: pl.BlockSpec(shape, lambda bi, h: (0, 0))
    return pl.pallas_call(
        functools.partial(_dilated_kernel, seq=s, rows=rows),
        grid=(b, n_heads),
        in_specs=[seq_spec(0), seq_spec(n_heads), seq_spec(2 * n_heads),
                  full((1, HEAD_DIM)), full((1, HEAD_DIM)),
                  full((s, HEAD_DIM)), full((s, HEAD_DIM))],
        out_specs=seq_spec(0),
        out_shape=jax.ShapeDtypeStruct((b, s, n_heads * HEAD_DIM), BF16),
        scratch_shapes=[pltpu.VMEM((s, HEAD_DIM), F32) for _ in range(9)],
        compiler_params=_params("parallel", "parallel"),
        name="dilated_attention",
    )(proj, proj, proj, gq, gk, cos_t, sin_t)


HALO = 16


def _conv_kernel(cur_ref, halo_ref, w_ref, o_ref, *, ts, q_scale):
    i = pl.program_id(1)
    j = pl.program_id(2)
    halo = jnp.where(i > 0, halo_ref[0].astype(F32), 0.0)
    xin = jnp.concatenate([halo, cur_ref[0].astype(F32)], axis=0)
    w = w_ref[...]
    y = None
    for tap in range(CONV_K):
        lo = HALO - (CONV_K - 1) + tap
        term = w[tap:tap + 1, :] * xin[lo:lo + ts, :]
        y = term if y is None else y + term
    y = y * jax.nn.sigmoid(y)
    scale = jnp.where(j == 0, q_scale, 1.0)
    for t in range(y.shape[1] // HEAD_DIM):
        cols = slice(t * HEAD_DIM, (t + 1) * HEAD_DIM)
        yt = y[:, cols]
        unit = yt * (lax.rsqrt(jnp.sum(yt * yt, axis=-1, keepdims=True) + NORM_EPS) * scale)
        o_ref[0, :, cols] = jnp.where(j < 2, unit, yt).astype(o_ref.dtype)


def _gdn_prologue(proj, col0, width, conv_w, ts):
    b, s, _ = proj.shape
    blk0 = col0 // width
    rows_per_halo = ts // HALO
    return pl.pallas_call(
        functools.partial(_conv_kernel, ts=ts, q_scale=HEAD_DIM ** -0.5),
        grid=(b, s // ts, 3),
        in_specs=[
            pl.BlockSpec((1, ts, width), lambda bi, i, j: (bi, i, blk0 + j)),
            pl.BlockSpec((1, HALO, width),
                         lambda bi, i, j: (bi, jnp.maximum(i * rows_per_halo - 1, 0), blk0 + j)),
            pl.BlockSpec((CONV_K, width), lambda bi, i, j: (0, j)),
        ],
        out_specs=pl.BlockSpec((1, ts, width), lambda bi, i, j: (bi, i, j)),
        out_shape=jax.ShapeDtypeStruct((b, s, 3 * width), BF16),
        compiler_params=_params("parallel", "parallel", "arbitrary"),
        name="gdn_prologue",
    )(proj, proj, conv_w)


def _dot_x3(a, b):
    ah, al = _split_bf16(a)
    bh, bl = _split_bf16(b)
    return _dot(ah, bh) + (_dot(ah, bl) + _dot(al, bh))


def _unit_lower_inverse(m):
    n = m.shape[0]
    row = lax.broadcasted_iota(jnp.int32, (n, n), 0)
    col = lax.broadcasted_iota(jnp.int32, (n, n), 1)
    inv = jnp.where(row == col, 1.0, 0.0) - m
    power = m
    k = 2
    while k < n:
        power = _dot_x3(power, power)
        inv = inv + _dot_x3(inv, power)
        k *= 2
    return inv


def _gdn_kernel(q_ref, k_ref, v_ref, z_ref, gate_ref, alog_ref, dtb_ref, gn_ref, o_ref, state_ref,
                *, n_heads):
    c = HEAD_DIM

    @pl.when(pl.program_id(1) == 0)
    def _():
        state_ref[...] = jnp.zeros(state_ref.shape, F32)

    gates = gate_ref[0]
    g = -jnp.exp(alog_ref[...]) * jax.nn.softplus(gates + dtb_ref[...])
    beta_all = jax.nn.sigmoid(gates)
    row = lax.broadcasted_iota(jnp.int32, (CHUNK, CHUNK), 0)
    col = lax.broadcasted_iota(jnp.int32, (CHUNK, CHUNK), 1)
    causal = row >= col
    strict = row > col
    tri = jnp.where(causal, 1.0, 0.0).astype(BF16)
    g_hi, g_lo = _split_bf16(g)
    gc = _dot(tri, g_hi) + _dot(tri, g_lo)
    gc_t = jnp.concatenate([gc, jnp.zeros_like(gc)], axis=0).T

    for h in range(n_heads):
        cols = slice(h * c, (h + 1) * c)
        gcol = gc[:, h:h + 1]
        grow = gc_t[h:h + 1, :CHUNK]
        bcol = beta_all[:, n_heads + h:n_heads + h + 1]
        decay_incl = jnp.exp(jnp.where(causal, gcol - grow, -jnp.inf))
        decay_strict = jnp.where(strict, decay_incl, 0.0)
        q = q_ref[0, :, cols]
        k = k_ref[0, :, cols]
        kf = k.astype(F32)
        kb = kf * bcol
        m = _dot_nt(kb.astype(BF16), k) * decay_strict
        inv = _unit_lower_inverse(m)
        egc = jnp.exp(gcol)
        rhs = jnp.concatenate([v_ref[0, :, cols].astype(F32) * bcol, kb * egc], axis=1)
        sol = _dot_x3(inv, rhs)
        u, w = sol[:, :c], sol[:, c:]
        attn = _dot_nt(q, k) * decay_incl
        g_last = gc[CHUNK - 1:CHUNK, h:h + 1]
        q_dec = (q.astype(F32) * egc).astype(BF16)
        k_dec = (kf * jnp.exp(g_last - gcol)).astype(BF16)
        state = state_ref[h]
        state_b = state.astype(BF16)
        v_new = u - _dot(w.astype(BF16), state_b)
        v_new_b = v_new.astype(BF16)
        o = _dot(q_dec, state_b) + _dot(attn.astype(BF16), v_new_b)
        state_ref[h] = state * jnp.exp(g_last) + _dot_tn(k_dec, v_new_b)
        z = z_ref[0, :, cols].astype(F32)
        o_ref[0, :, cols] = (_rms_rows(o, gn_ref[...]) * (z * jax.nn.sigmoid(z))).astype(o_ref.dtype)


def _gated_delta(qkv, proj, z_col0, gates, a_log, dt_bias, gn, n_heads):
    b, s, _ = qkv.shape
    width = n_heads * HEAD_DIM
    z_blk = z_col0 // width
    chunk_spec = lambda blk: pl.BlockSpec((1, CHUNK, width), lambda bi, ci: (bi, ci, blk))
    row_spec = pl.BlockSpec((1, LANES), lambda bi, ci: (0, 0))
    return pl.pallas_call(
        functools.partial(_gdn_kernel, n_heads=n_heads),
        grid=(b, s // CHUNK),
        in_specs=[chunk_spec(0), chunk_spec(1), chunk_spec(2), chunk_spec(z_blk),
                  pl.BlockSpec((1, CHUNK, LANES), lambda bi, ci: (bi, ci, 0)),
                  row_spec, row_spec, row_spec],
        out_specs=chunk_spec(0),
        out_shape=jax.ShapeDtypeStruct((b, s, width), BF16),
        scratch_shapes=[pltpu.VMEM((n_heads, HEAD_DIM, HEAD_DIM), F32)],
        compiler_params=_params("parallel", "arbitrary"),
        name="gated_delta_rule",
    )(qkv, qkv, qkv, proj, gates, a_log, dt_bias, gn)


TM_PROJ, TN_PROJ = 512, 512
TM_MLP, TF_MLP = 512, 512
TQ_DIFF, TK_DIFF = 256, 512
TS_ELEM = 512


def _lane_row(v):
    return jnp.zeros((1, LANES), F32).at[0, :v.shape[0]].set(v.astype(F32))


def _diff_delta_layer(xf, b, s, layer_idx, norm, w_in, a_q_norm, a_k_norm, a_lambda, a_sub_norm,
                      b_conv, b_a_log, b_dt_bias, b_out_norm, w_out, cos_a, sin_a):
    t, d = xf.shape
    a_width = d // 2
    b_width = d - a_width
    a_heads = a_width // HEAD_DIM
    b_heads = b_width // HEAD_DIM
    main = 3 * a_width + 4 * b_width
    w_main = w_in[:, :main].astype(BF16)
    w_gate = jnp.zeros((d, LANES), BF16).at[:, :2 * b_heads].set(w_in[:, main:].astype(BF16))
    g = norm[None]
    proj = _norm_matmul(xf, g, w_main, BF16, TM_PROJ, TN_PROJ).reshape(b, s, main)
    gates = _norm_matmul(xf, g, w_gate, F32, TM_PROJ, LANES).reshape(b, s, LANES)

    lam_init = 0.8 - 0.6 * math.exp(-0.3 * layer_idx)
    qk_gain = jnp.concatenate([jnp.tile(a_q_norm * A_SUB ** -0.5, 2 * a_heads),
                               jnp.tile(a_k_norm, 2 * a_heads)])[None]
    qk = _qk64_prologue(proj, qk_gain, cos_a, sin_a, 2 * a_width, TS_ELEM, 512)
    oa = _diff_attention(qk, proj, 2 * a_width, a_lambda, (a_sub_norm * (1.0 - lam_init))[None],
                         lam_init, a_heads, TQ_DIFF, TK_DIFF)

    qkv = _gdn_prologue(proj, 3 * a_width, b_width, b_conv, TS_ELEM)
    ob = _gated_delta(qkv, proj, 3 * a_width + 3 * b_width, gates, _lane_row(b_a_log),
                      _lane_row(b_dt_bias), b_out_norm[None], b_heads)

    w_out = w_out.astype(BF16)
    return _out_proj(xf, [oa.reshape(t, a_width), ob.reshape(t, b_width)],
                     [w_out[:a_width], w_out[a_width:]], TM_PROJ, TN_PROJ)


def _dilated_layer(xf, b, s, norm, w_in, q_norm, k_norm, w_out, cos_c, sin_c):
    t, d = xf.shape
    width = w_in.shape[1] // 3
    proj = _norm_matmul(xf, norm[None], w_in.astype(BF16), BF16, TM_PROJ, TN_PROJ)
    o = _dilated_attention(proj.reshape(b, s, 3 * width), (q_norm * HEAD_DIM ** -0.5)[None],
                           k_norm[None], cos_c, sin_c, width // HEAD_DIM, TS_ELEM)
    return _out_proj(xf, [o.reshape(t, width)], [w_out.astype(BF16)], TM_PROJ, TN_PROJ)


def kernel(x, ab_norm, ab_w_in, a_q_norm, a_k_norm, a_lambda, a_sub_norm, b_conv, b_a_log, b_dt_bias,
           b_out_norm, ab_w_out, c_norm, c_w_in, c_q_norm, c_k_norm, c_w_out, mlp_norm, mlp_w1, mlp_w2):
    b, s, d = x.shape
    depth = mlp_w1.shape[0]
    cos_a, sin_a = _rope_tables(s, A_SUB)
    cos_c, sin_c = _rope_tables(s, HEAD_DIM)
    xf = x.reshape(b * s, d)
    for l in range(depth):
        i = l // 2
        if l % 2 == 0:
            xf = _diff_delta_layer(xf, b, s, l, ab_norm[i], ab_w_in[i], a_q_norm[i], a_k_norm[i],
                                   a_lambda[i], a_sub_norm[i], b_conv[i], b_a_log[i], b_dt_bias[i],
                                   b_out_norm[i], ab_w_out[i], cos_a, sin_a)
        else:
            xf = _dilated_layer(xf, b, s, c_norm[i], c_w_in[i], c_q_norm[i], c_k_norm[i], c_w_out[i],
                                cos_c, sin_c)
        xf = _mlp(xf, mlp_norm[l][None], mlp_w1[l].astype(BF16), mlp_w2[l].astype(BF16),
                  TM_MLP, TF_MLP)
    return xf.reshape(b, s, d)
```

```python
import functools
import math

import jax
import jax.numpy as jnp
from jax import lax
from jax.experimental import pallas as pl
from jax.experimental.pallas import tpu as pltpu

F32 = jnp.float32
BF16 = jnp.bfloat16

HEAD_DIM = 128
A_SUB = HEAD_DIM // 2
CONV_K = 4
CHUNK = 64
ROPE_THETA = 10000.0
NORM_EPS = 1e-6
C_BRANCHES = ((128, 1), (512, 4), (2048, 16))
LANES = 128
VMEM_LIMIT_BYTES = 56 * 1024 * 1024


def _params(*semantics):
    return pltpu.CompilerParams(dimension_semantics=semantics, vmem_limit_bytes=VMEM_LIMIT_BYTES)


def _rms_rows(x, gain):
    ms = jnp.mean(x * x, axis=-1, keepdims=True)
    return x * lax.rsqrt(ms + NORM_EPS) * gain


def _dot(a, b):
    return jnp.dot(a, b, preferred_element_type=F32)


def _dot_nt(a, b):
    return lax.dot_general(a, b, (((1,), (1,)), ((), ())), preferred_element_type=F32)


def _dot_tn(a, b):
    return lax.dot_general(a, b, (((0,), (0,)), ((), ())), preferred_element_type=F32)


def _split_bf16(x):
    hi = x.astype(BF16)
    lo = (x - hi.astype(F32)).astype(BF16)
    return hi, lo


def _norm_matmul_kernel(x_ref, g_ref, w_ref, o_ref, xn_ref):
    @pl.when(pl.program_id(1) == 0)
    def _():
        xn_ref[...] = _rms_rows(x_ref[...], g_ref[...]).astype(BF16)

    o_ref[...] = _dot(xn_ref[...], w_ref[...]).astype(o_ref.dtype)


def _norm_matmul(x, g, w, out_dtype, tm, tn):
    t, d = x.shape
    n = w.shape[1]
    tn = min(tn, n)
    return pl.pallas_call(
        _norm_matmul_kernel,
        grid=(t // tm, n // tn),
        in_specs=[
            pl.BlockSpec((tm, d), lambda i, j: (i, 0)),
            pl.BlockSpec((1, d), lambda i, j: (0, 0)),
            pl.BlockSpec((d, tn), lambda i, j: (0, j)),
        ],
        out_specs=pl.BlockSpec((tm, tn), lambda i, j: (i, j)),
        out_shape=jax.ShapeDtypeStruct((t, n), out_dtype),
        scratch_shapes=[pltpu.VMEM((tm, d), BF16)],
        compiler_params=_params("parallel", "arbitrary"),
        name="norm_matmul",
    )(x, g, w)


def _out_proj_kernel(*refs, n_in):
    x_ref = refs[0]
    o_refs = refs[1:1 + n_in]
    w_refs = refs[1 + n_in:1 + 2 * n_in]
    out_ref = refs[1 + 2 * n_in]
    acc = x_ref[...]
    for o_ref, w_ref in zip(o_refs, w_refs):
        acc = acc + _dot(o_ref[...], w_ref[...])
    out_ref[...] = acc


def _out_proj(x, os_, ws, tm, tn):
    t, d = x.shape
    n_in = len(os_)
    in_specs = [pl.BlockSpec((tm, tn), lambda i, j: (i, j))]
    in_specs += [pl.BlockSpec((tm, o.shape[1]), lambda i, j: (i, 0)) for o in os_]
    in_specs += [pl.BlockSpec((w.shape[0], tn), lambda i, j: (0, j)) for w in ws]
    return pl.pallas_call(
        functools.partial(_out_proj_kernel, n_in=n_in),
        grid=(t // tm, d // tn),
        in_specs=in_specs,
        out_specs=pl.BlockSpec((tm, tn), lambda i, j: (i, j)),
        out_shape=jax.ShapeDtypeStruct((t, d), F32),
        compiler_params=_params("parallel", "arbitrary"),
        name="out_proj",
    )(x, *os_, *ws)


def _mlp_kernel(x_ref, g_ref, w1_ref, w2_ref, o_ref, xn_ref):
    @pl.when(pl.program_id(1) == 0)
    def _():
        x = x_ref[...]
        xn_ref[...] = _rms_rows(x, g_ref[...]).astype(BF16)
        o_ref[...] = x

    h = jnp.maximum(_dot(xn_ref[...], w1_ref[...]), 0.0)
    o_ref[...] += _dot((h * h).astype(BF16), w2_ref[...])


def _mlp(x, g, w1, w2, tm, tf):
    t, d = x.shape
    f = w1.shape[1]
    return pl.pallas_call(
        _mlp_kernel,
        grid=(t // tm, f // tf),
        in_specs=[
            pl.BlockSpec((tm, d), lambda i, j: (i, 0)),
            pl.BlockSpec((1, d), lambda i, j: (0, 0)),
            pl.BlockSpec((d, tf), lambda i, j: (0, j)),
            pl.BlockSpec((tf, d), lambda i, j: (j, 0)),
        ],
        out_specs=pl.BlockSpec((tm, d), lambda i, j: (i, 0)),
        out_shape=jax.ShapeDtypeStruct((t, d), F32),
        scratch_shapes=[pltpu.VMEM((tm, d), BF16)],
        compiler_params=_params("parallel", "arbitrary"),
        name="mlp",
    )(x, g, w1, w2)


def _rope_tables(seq, dim):
    half = dim // 2
    inv = 1.0 / (ROPE_THETA ** (jnp.arange(0, dim, 2, dtype=F32) / dim))
    ang = jnp.arange(seq, dtype=F32)[:, None] * inv[None, :]
    cos, sin = jnp.cos(ang), jnp.sin(ang)
    reps = LANES // dim
    cos_t = jnp.tile(jnp.concatenate([cos, cos], axis=-1), (1, reps))
    sin_t = jnp.tile(jnp.concatenate([-sin, sin], axis=-1), (1, reps))
    return cos_t, sin_t


def _rotate_half(y, dim):
    half = dim // 2
    if dim == LANES:
        return pltpu.roll(y, half, axis=1)
    lane = lax.broadcasted_iota(jnp.int32, y.shape, 1)
    first = (lane % dim) < half
    return jnp.where(first, pltpu.roll(y, LANES - half, axis=1), pltpu.roll(y, half, axis=1))


def _qk64_kernel(p_ref, g_ref, cos_ref, sin_ref, grp_ref, o_ref):
    cos = cos_ref[...]
    sin = sin_ref[...]
    grp = grp_ref[...]
    for t in range(p_ref.shape[2] // LANES):
        cols = slice(t * LANES, (t + 1) * LANES)
        x = p_ref[0, :, cols].astype(F32)
        hi, lo = _split_bf16(x * x)
        ms = (_dot(hi, grp) + _dot(lo, grp)) * (1.0 / A_SUB)
        y = x * lax.rsqrt(ms + NORM_EPS) * g_ref[:, cols]
        o_ref[0, :, cols] = (y * cos + _rotate_half(y, A_SUB) * sin).astype(o_ref.dtype)


def _qk64_prologue(proj, gain, cos_t, sin_t, width, ts, cw):
    b, s, _ = proj.shape
    lane = jnp.arange(LANES)
    grp = (lane[:, None] // A_SUB == lane[None, :] // A_SUB).astype(BF16)
    return pl.pallas_call(
        _qk64_kernel,
        grid=(b, s // ts, width // cw),
        in_specs=[
            pl.BlockSpec((1, ts, cw), lambda bi, i, j: (bi, i, j)),
            pl.BlockSpec((1, cw), lambda bi, i, j: (0, j)),
            pl.BlockSpec((ts, LANES), lambda bi, i, j: (i, 0)),
            pl.BlockSpec((ts, LANES), lambda bi, i, j: (i, 0)),
            pl.BlockSpec((LANES, LANES), lambda bi, i, j: (0, 0)),
        ],
        out_specs=pl.BlockSpec((1, ts, cw), lambda bi, i, j: (bi, i, j)),
        out_shape=jax.ShapeDtypeStruct((b, s, width), BF16),
        compiler_params=_params("parallel", "parallel", "arbitrary"),
        name="qk64_prologue",
    )(proj, gain, cos_t, sin_t, grp)


DIFF_HEADS_PER_STEP = 2


def _diff_attn_kernel(q_ref, k_ref, v_ref, lam_ref, gn_ref, o_ref, q2_ref, m_ref, l_ref, acc_ref,
                      *, seq, t, hp, lam_init):
    heads = range(hp)
    cols = [slice(h * HEAD_DIM, (h + 1) * HEAD_DIM) for h in heads]
    lv = lam_ref[...]
    lam = (jnp.exp(jnp.sum(lv[0:1] * lv[1:2], axis=1, keepdims=True))
           - jnp.exp(jnp.sum(lv[2:3] * lv[3:4], axis=1, keepdims=True)) + lam_init)
    key = lax.broadcasted_iota(jnp.int32, (t, 2 * t), 0)
    qry = lax.broadcasted_iota(jnp.int32, (t, 2 * t), 1)
    diag_mask = key <= jnp.where(qry >= t, qry - t, qry)
    lane = lax.broadcasted_iota(jnp.int32, (t, HEAD_DIM), 1)

    def kv_tile(j, masked):
        rows = pl.ds(pl.multiple_of(j * t, t), t)
        s = [_dot_nt(k_ref[0, rows, cols[h]], q2_ref[h]) for h in heads]
        if masked:
            s = [jnp.where(diag_mask, sh, -jnp.inf) for sh in s]
        m_prev = [m_ref[h] for h in heads]
        m_new = [jnp.maximum(m_prev[h], jnp.max(s[h], axis=0, keepdims=True)) for h in heads]
        alpha = [jnp.exp(m_prev[h] - m_new[h]) for h in heads]
        p = [jnp.exp(s[h] - m_new[h]) for h in heads]
        pv = [_dot_tn(v_ref[0, rows, cols[h]], p[h].astype(BF16)) for h in heads]
        for h in heads:
            l_ref[h] = alpha[h] * l_ref[h] + jnp.sum(p[h], axis=0, keepdims=True)
            acc_ref[h] = alpha[h] * acc_ref[h] + pv[h]
            m_ref[h] = m_new[h]

    def q_block(i, carry):
        rows = pl.ds(pl.multiple_of(i * t, t), t)
        for h in heads:
            q = q_ref[0, rows, cols[h]]
            zero = jnp.zeros_like(q)
            q2_ref[h, :t, :] = jnp.where(lane < A_SUB, q, zero)
            q2_ref[h, t:, :] = jnp.where(lane < A_SUB, zero, q)
        m_ref[...] = jnp.full(m_ref.shape, -jnp.inf, F32)
        l_ref[...] = jnp.zeros(l_ref.shape, F32)
        acc_ref[...] = jnp.zeros(acc_ref.shape, F32)

        def visible(j, c):
            kv_tile(j, False)
            return c

        lax.fori_loop(0, i, visible, 0)
        kv_tile(i, True)
        for h in heads:
            o12 = acc_ref[h] / l_ref[h]
            o = (o12[:, :t] - lam * o12[:, t:]).T
            o_ref[0, rows, cols[h]] = _rms_rows(o, gn_ref[...]).astype(o_ref.dtype)
        return carry

    lax.fori_loop(0, seq // t, q_block, 0)


def _diff_attention(qk, proj, v_col0, a_lambda, gn, lam_init, n_heads, t):
    b, s, _ = qk.shape
    hp = DIFF_HEADS_PER_STEP
    width = hp * HEAD_DIM
    kernel = functools.partial(_diff_attn_kernel, seq=s, t=t, hp=hp, lam_init=lam_init)
    k_blk0 = n_heads // hp
    v_blk0 = v_col0 // width
    seq_spec = lambda off: pl.BlockSpec((1, s, width), lambda bi, g: (bi, 0, off + g))
    return pl.pallas_call(
        kernel,
        grid=(b, n_heads // hp),
        in_specs=[
            seq_spec(0), seq_spec(k_blk0), seq_spec(v_blk0),
            pl.BlockSpec((4, A_SUB), lambda bi, g: (0, 0)),
            pl.BlockSpec((1, HEAD_DIM), lambda bi, g: (0, 0)),
        ],
        out_specs=seq_spec(0),
        out_shape=jax.ShapeDtypeStruct((b, s, n_heads * HEAD_DIM), BF16),
        scratch_shapes=[
            pltpu.VMEM((hp, 2 * t, HEAD_DIM), BF16),
            pltpu.VMEM((hp, 1, 2 * t), F32),
            pltpu.VMEM((hp, 1, 2 * t), F32),
            pltpu.VMEM((hp, HEAD_DIM, 2 * t), F32),
        ],
        compiler_params=_params("parallel", "parallel"),
        name="diff_attention",
    )(qk, qk, proj, a_lambda, gn)


DILATED_UNROLL = 4


def _dilated_kernel(q_ref, k_ref, v_ref, gq_ref, gk_ref, cos_ref, sin_ref, o_ref,
                    qf_ref, kf_ref, vf_ref, ob0, ob1, ob2, ls0, ls1, ls2, *, seq, rows, unroll):
    def prep(c, carry):
        sl = pl.ds(pl.multiple_of(c * rows, rows), rows)
        cos = cos_ref[sl, :]
        sin = sin_ref[sl, :]
        for src, g_ref, dst in ((q_ref, gq_ref, qf_ref), (k_ref, gk_ref, kf_ref)):
            y = _rms_rows(src[0, sl, :].astype(F32), g_ref[...])
            dst[sl, :] = y * cos + _rotate_half(y, HEAD_DIM) * sin
        vf_ref[sl, :] = v_ref[0, sl, :].astype(F32)
        return carry

    lax.fori_loop(0, seq // rows, prep, 0)

    for (window, dil), ob_ref, ls_ref in zip(C_BRANCHES, (ob0, ob1, ob2), (ls0, ls1, ls2)):
        hops = window // dil
        span = dil * hops
        n_blk = seq // span
        qi = lax.broadcasted_iota(jnp.int32, (hops, hops), 0)
        kj = lax.broadcasted_iota(jnp.int32, (hops, hops), 1)
        own_mask = kj <= qi
        prev_mask = kj >= qi

        def blocks(it, carry, dil=dil, hops=hops, span=span, n_blk=n_blk, ob_ref=ob_ref,
                   ls_ref=ls_ref, own_mask=own_mask, prev_mask=prev_mask):
            us = range(unroll)
            idx = [it * unroll + u for u in us]
            r = [i // n_blk for i in idx]
            n = [i - ri * n_blk for i, ri in zip(idx, r)]
            base = [ri + ni * span for ri, ni in zip(r, n)]
            own = [pl.ds(bs, hops, stride=dil) for bs in base]
            prev = [pl.ds(jnp.maximum(bs - span, ri), hops, stride=dil) for bs, ri in zip(base, r)]
            q = [qf_ref[own[u], :].astype(BF16) for u in us]
            s1 = [jnp.where(own_mask, _dot_nt(q[u], kf_ref[own[u], :].astype(BF16)), -jnp.inf)
                  for u in us]
            s0 = [jnp.where(jnp.logical_and(prev_mask, n[u] > 0),
                            _dot_nt(q[u], kf_ref[prev[u], :].astype(BF16)), -jnp.inf) for u in us]
            m = [jnp.maximum(jnp.max(s1[u], axis=1, keepdims=True),
                             jnp.max(s0[u], axis=1, keepdims=True)) for u in us]
            p1 = [jnp.exp(s1[u] - m[u]) for u in us]
            p0 = [jnp.exp(s0[u] - m[u]) for u in us]
            den = [jnp.sum(p1[u], axis=1, keepdims=True) + jnp.sum(p0[u], axis=1, keepdims=True)
                   for u in us]
            o = [_dot(p1[u].astype(BF16), vf_ref[own[u], :].astype(BF16))
                 + _dot(p0[u].astype(BF16), vf_ref[prev[u], :].astype(BF16)) for u in us]
            for u in us:
                ob_ref[own[u], :] = o[u] / den[u]
                ls_ref[own[u], :] = jnp.broadcast_to(m[u] + jnp.log(den[u]), (hops, HEAD_DIM))
            return carry

        lax.fori_loop(0, dil * n_blk // unroll, blocks, 0)

    def merge(c, carry):
        sl = pl.ds(pl.multiple_of(c * rows, rows), rows)
        l0, l1, l2 = ls0[sl, :], ls1[sl, :], ls2[sl, :]
        m = jnp.maximum(jnp.maximum(l0, l1), l2)
        e0, e1, e2 = jnp.exp(l0 - m), jnp.exp(l1 - m), jnp.exp(l2 - m)
        o = (e0 * ob0[sl, :] + e1 * ob1[sl, :] + e2 * ob2[sl, :]) / (e0 + e1 + e2)
        o_ref[0, sl, :] = o.astype(o_ref.dtype)
        return carry

    lax.fori_loop(0, seq // rows, merge, 0)


def _dilated_attention(proj, gq, gk, cos_t, sin_t, n_heads, rows):
    b, s, _ = proj.shape
    seq_spec = lambda off: pl.BlockSpec((1, s, HEAD_DIM), lambda bi, h: (bi, 0, off + h))
    full = lambda shape: pl.BlockSpec(shape, lambda bi, h: (0, 0))
    return pl.pallas_call(
        functools.partial(_dilated_kernel, seq=s, rows=rows, unroll=DILATED_UNROLL),
        grid=(b, n_heads),
        in_specs=[seq_spec(0), seq_spec(n_heads), seq_spec(2 * n_heads),
                  full((1, HEAD_DIM)), full((1, HEAD_DIM)),
                  full((s, HEAD_DIM)), full((s, HEAD_DIM))],
        out_specs=seq_spec(0),
        out_shape=jax.ShapeDtypeStruct((b, s, n_heads * HEAD_DIM), BF16),
        scratch_shapes=[pltpu.VMEM((s, HEAD_DIM), F32) for _ in range(9)],
        compiler_params=_params("parallel", "parallel"),
        name="dilated_attention",
    )(proj, proj, proj, gq, gk, cos_t, sin_t)


HALO = 16


def _conv_kernel(cur_ref, halo_ref, w_ref, o_ref, *, ts, q_scale):
    i = pl.program_id(1)
    j = pl.program_id(2)
    halo = jnp.where(i > 0, halo_ref[0].astype(F32), 0.0)
    xin = jnp.concatenate([halo, cur_ref[0].astype(F32)], axis=0)
    w = w_ref[...]
    y = None
    for tap in range(CONV_K):
        lo = HALO - (CONV_K - 1) + tap
        term = w[tap:tap + 1, :] * xin[lo:lo + ts, :]
        y = term if y is None else y + term
    y = y * jax.nn.sigmoid(y)
    scale = jnp.where(j == 0, q_scale, 1.0)
    for t in range(y.shape[1] // HEAD_DIM):
        cols = slice(t * HEAD_DIM, (t + 1) * HEAD_DIM)
        yt = y[:, cols]
        unit = yt * (lax.rsqrt(jnp.sum(yt * yt, axis=-1, keepdims=True) + NORM_EPS) * scale)
        o_ref[0, :, cols] = jnp.where(j < 2, unit, yt).astype(o_ref.dtype)


def _gdn_prologue(proj, col0, width, conv_w, ts):
    b, s, _ = proj.shape
    blk0 = col0 // width
    rows_per_halo = ts // HALO
    return pl.pallas_call(
        functools.partial(_conv_kernel, ts=ts, q_scale=HEAD_DIM ** -0.5),
        grid=(b, s // ts, 3),
        in_specs=[
            pl.BlockSpec((1, ts, width), lambda bi, i, j: (bi, i, blk0 + j)),
            pl.BlockSpec((1, HALO, width),
                         lambda bi, i, j: (bi, jnp.maximum(i * rows_per_halo - 1, 0), blk0 + j)),
            pl.BlockSpec((CONV_K, width), lambda bi, i, j: (0, j)),
        ],
        out_specs=pl.BlockSpec((1, ts, width), lambda bi, i, j: (bi, i, j)),
        out_shape=jax.ShapeDtypeStruct((b, s, 3 * width), BF16),
        compiler_params=_params("parallel", "parallel", "arbitrary"),
        name="gdn_prologue",
    )(proj, proj, conv_w)


def _dot_x3(a, b):
    (ah, al), (bh, bl) = a, b
    return _dot(ah, bh) + (_dot(ah, bl) + _dot(al, bh))


def _unit_lower_inverses(ms):
    n = ms[0].shape[0]
    row = lax.broadcasted_iota(jnp.int32, (n, n), 0)
    col = lax.broadcasted_iota(jnp.int32, (n, n), 1)
    eye = jnp.where(row == col, 1.0, 0.0)
    invs = [eye - m for m in ms]
    powers = [_split_bf16(m) for m in ms]
    k = 2
    while k < n:
        powers = [_split_bf16(_dot_x3(p, p)) for p in powers]
        invs = [inv + _dot_x3(_split_bf16(inv), p) for inv, p in zip(invs, powers)]
        k *= 2
    return invs


def _gdn_kernel(q_ref, k_ref, v_ref, z_ref, gate_ref, alog_ref, dtb_ref, gn_ref, o_ref, state_ref,
                *, n_heads):
    c = HEAD_DIM
    heads = range(n_heads)

    @pl.when(pl.program_id(1) == 0)
    def _():
        state_ref[...] = jnp.zeros(state_ref.shape, F32)

    gates = gate_ref[0]
    g = -jnp.exp(alog_ref[...]) * jax.nn.softplus(gates + dtb_ref[...])
    beta_all = jax.nn.sigmoid(gates)
    row = lax.broadcasted_iota(jnp.int32, (CHUNK, CHUNK), 0)
    col = lax.broadcasted_iota(jnp.int32, (CHUNK, CHUNK), 1)
    causal = row >= col
    strict = row > col
    tri = jnp.where(causal, 1.0, 0.0).astype(BF16)
    g_hi, g_lo = _split_bf16(g)
    gc = _dot(tri, g_hi) + _dot(tri, g_lo)
    gc_t = jnp.concatenate([gc, jnp.zeros_like(gc)], axis=0).T

    cols = [slice(h * c, (h + 1) * c) for h in heads]
    gcol = [gc[:, h:h + 1] for h in heads]
    bcol = [beta_all[:, n_heads + h:n_heads + h + 1] for h in heads]
    decay_incl = [jnp.exp(jnp.where(causal, gcol[h] - gc_t[h:h + 1, :CHUNK], -jnp.inf)) for h in heads]
    q = [q_ref[0, :, cols[h]] for h in heads]
    k = [k_ref[0, :, cols[h]] for h in heads]
    kf = [k[h].astype(F32) for h in heads]
    kb = [kf[h] * bcol[h] for h in heads]
    m = [_dot_nt(kb[h].astype(BF16), k[h]) * jnp.where(strict, decay_incl[h], 0.0) for h in heads]
    inv = _unit_lower_inverses(m)
    egc = [jnp.exp(gcol[h]) for h in heads]
    rhs = [jnp.concatenate([v_ref[0, :, cols[h]].astype(F32) * bcol[h], kb[h] * egc[h]], axis=1)
           for h in heads]
    sol = [_dot_x3(_split_bf16(inv[h]), _split_bf16(rhs[h])) for h in heads]
    attn = [(_dot_nt(q[h], k[h]) * decay_incl[h]).astype(BF16) for h in heads]
    g_last = [gc[CHUNK - 1:CHUNK, h:h + 1] for h in heads]
    q_dec = [(q[h].astype(F32) * egc[h]).astype(BF16) for h in heads]
    k_dec = [(kf[h] * jnp.exp(g_last[h] - gcol[h])).astype(BF16) for h in heads]
    state = [state_ref[h] for h in heads]
    state_b = [s.astype(BF16) for s in state]
    v_new = [(sol[h][:, :c] - _dot(sol[h][:, c:].astype(BF16), state_b[h])).astype(BF16) for h in heads]
    o = [_dot(q_dec[h], state_b[h]) + _dot(attn[h], v_new[h]) for h in heads]
    for h in heads:
        state_ref[h] = state[h] * jnp.exp(g_last[h]) + _dot_tn(k_dec[h], v_new[h])
    for h in heads:
        z = z_ref[0, :, cols[h]].astype(F32)
        o_ref[0, :, cols[h]] = (_rms_rows(o[h], gn_ref[...]) * (z * jax.nn.sigmoid(z))).astype(o_ref.dtype)


def _gated_delta(qkv, proj, z_col0, gates, a_log, dt_bias, gn, n_heads):
    b, s, _ = qkv.shape
    width = n_heads * HEAD_DIM
    z_blk = z_col0 // width
    chunk_spec = lambda blk: pl.BlockSpec((1, CHUNK, width), lambda bi, ci: (bi, ci, blk))
    row_spec = pl.BlockSpec((1, LANES), lambda bi, ci: (0, 0))
    return pl.pallas_call(
        functools.partial(_gdn_kernel, n_heads=n_heads),
        grid=(b, s // CHUNK),
        in_specs=[chunk_spec(0), chunk_spec(1), chunk_spec(2), chunk_spec(z_blk),
                  pl.BlockSpec((1, CHUNK, LANES), lambda bi, ci: (bi, ci, 0)),
                  row_spec, row_spec, row_spec],
        out_specs=chunk_spec(0),
        out_shape=jax.ShapeDtypeStruct((b, s, width), BF16),
        scratch_shapes=[pltpu.VMEM((n_heads, HEAD_DIM, HEAD_DIM), F32)],
        compiler_params=_params("parallel", "arbitrary"),
        name="gated_delta_rule",
    )(qkv, qkv, qkv, proj, gates, a_log, dt_bias, gn)


TM_PROJ, TN_PROJ = 512, 512
TM_MLP, TF_MLP = 512, 512
T_DIFF = 256
TS_ELEM = 512


def _lane_row(v):
    return jnp.zeros((1, LANES), F32).at[0, :v.shape[0]].set(v.astype(F32))


def _diff_delta_layer(xf, b, s, layer_idx, norm, w_in, a_q_norm, a_k_norm, a_lambda, a_sub_norm,
                      b_conv, b_a_log, b_dt_bias, b_out_norm, w_out, cos_a, sin_a):
    t, d = xf.shape
    a_width = d // 2
    b_width = d - a_width
    a_heads = a_width // HEAD_DIM
    b_heads = b_width // HEAD_DIM
    main = 3 * a_width + 4 * b_width
    w_main = w_in[:, :main].astype(BF16)
    w_gate = jnp.zeros((d, LANES), BF16).at[:, :2 * b_heads].set(w_in[:, main:].astype(BF16))
    g = norm[None]
    proj = _norm_matmul(xf, g, w_main, BF16, TM_PROJ, TN_PROJ).reshape(b, s, main)
    gates = _norm_matmul(xf, g, w_gate, F32, TM_PROJ, LANES).reshape(b, s, LANES)

    lam_init = 0.8 - 0.6 * math.exp(-0.3 * layer_idx)
    qk_gain = jnp.concatenate([jnp.tile(a_q_norm * A_SUB ** -0.5, 2 * a_heads),
                               jnp.tile(a_k_norm, 2 * a_heads)])[None]
    qk = _qk64_prologue(proj, qk_gain, cos_a, sin_a, 2 * a_width, TS_ELEM, 512)
    oa = _diff_attention(qk, proj, 2 * a_width, a_lambda, (a_sub_norm * (1.0 - lam_init))[None],
                         lam_init, a_heads, T_DIFF)

    qkv = _gdn_prologue(proj, 3 * a_width, b_width, b_conv, TS_ELEM)
    ob = _gated_delta(qkv, proj, 3 * a_width + 3 * b_width, gates, _lane_row(b_a_log),
                      _lane_row(b_dt_bias), b_out_norm[None], b_heads)

    w_out = w_out.astype(BF16)
    return _out_proj(xf, [oa.reshape(t, a_width), ob.reshape(t, b_width)],
                     [w_out[:a_width], w_out[a_width:]], TM_PROJ, TN_PROJ)


def _dilated_layer(xf, b, s, norm, w_in, q_norm, k_norm, w_out, cos_c, sin_c):
    t, d = xf.shape
    width = w_in.shape[1] // 3
    proj = _norm_matmul(xf, norm[None], w_in.astype(BF16), BF16, TM_PROJ, TN_PROJ)
    o = _dilated_attention(proj.reshape(b, s, 3 * width), (q_norm * HEAD_DIM ** -0.5)[None],
                           k_norm[None], cos_c, sin_c, width // HEAD_DIM, TS_ELEM)
    return _out_proj(xf, [o.reshape(t, width)], [w_out.astype(BF16)], TM_PROJ, TN_PROJ)


def kernel(x, ab_norm, ab_w_in, a_q_norm, a_k_norm, a_lambda, a_sub_norm, b_conv, b_a_log, b_dt_bias,
           b_out_norm, ab_w_out, c_norm, c_w_in, c_q_norm, c_k_norm, c_w_out, mlp_norm, mlp_w1, mlp_w2):
    b, s, d = x.shape
    depth = mlp_w1.shape[0]
    cos_a, sin_a = _rope_tables(s, A_SUB)
    cos_c, sin_c = _rope_tables(s, HEAD_DIM)
    xf = x.reshape(b * s, d)
    for l in range(depth):
        i = l // 2
        if l % 2 == 0:
            xf = _diff_delta_layer(xf, b, s, l, ab_norm[i], ab_w_in[i], a_q_norm[i], a_k_norm[i],
                                   a_lambda[i], a_sub_norm[i], b_conv[i], b_a_log[i], b_dt_bias[i],
                                   b_out_norm[i], ab_w_out[i], cos_a, sin_a)
        else:
            xf = _dilated_layer(xf, b, s, c_norm[i], c_w_in[i], c_q_norm[i], c_k_norm[i], c_w_out[i],
                                cos_c, sin_c)
        xf = _mlp(xf, mlp_norm[l][None], mlp_w1[l].astype(BF16), mlp_w2[l].astype(BF16),
                  TM_MLP, TF_MLP)
    return xf.reshape(b, s, d)
```

```python
import functools
import math

import jax
import jax.numpy as jnp
from jax import lax
from jax.experimental import pallas as pl
from jax.experimental.pallas import tpu as pltpu

F32 = jnp.float32
BF16 = jnp.bfloat16

HEAD_DIM = 128
A_SUB = HEAD_DIM // 2
CONV_K = 4
CHUNK = 64
ROPE_THETA = 10000.0
NORM_EPS = 1e-6
C_BRANCHES = ((128, 1), (512, 4), (2048, 16))
LANES = 128
VMEM_LIMIT_BYTES = 56 * 1024 * 1024


def _params(*semantics):
    return pltpu.CompilerParams(dimension_semantics=semantics, vmem_limit_bytes=VMEM_LIMIT_BYTES)


def _rms_rows(x, gain):
    ms = jnp.mean(x * x, axis=-1, keepdims=True)
    return x * lax.rsqrt(ms + NORM_EPS) * gain


def _dot(a, b):
    return jnp.dot(a, b, preferred_element_type=F32)


def _dot_nt(a, b):
    return lax.dot_general(a, b, (((1,), (1,)), ((), ())), preferred_element_type=F32)


def _dot_tn(a, b):
    return lax.dot_general(a, b, (((0,), (0,)), ((), ())), preferred_element_type=F32)


def _split_bf16(x):
    hi = x.astype(BF16)
    lo = (x - hi.astype(F32)).astype(BF16)
    return hi, lo


def _norm_matmul_kernel(x_ref, g_ref, w_ref, o_ref, xn_ref):
    @pl.when(pl.program_id(1) == 0)
    def _():
        xn_ref[...] = _rms_rows(x_ref[...], g_ref[...]).astype(BF16)

    o_ref[...] = _dot(xn_ref[...], w_ref[...]).astype(o_ref.dtype)


def _norm_matmul(x, g, w, out_dtype, tm, tn):
    t, d = x.shape
    n = w.shape[1]
    tn = min(tn, n)
    return pl.pallas_call(
        _norm_matmul_kernel,
        grid=(t // tm, n // tn),
        in_specs=[
            pl.BlockSpec((tm, d), lambda i, j: (i, 0)),
            pl.BlockSpec((1, d), lambda i, j: (0, 0)),
            pl.BlockSpec((d, tn), lambda i, j: (0, j)),
        ],
        out_specs=pl.BlockSpec((tm, tn), lambda i, j: (i, j)),
        out_shape=jax.ShapeDtypeStruct((t, n), out_dtype),
        scratch_shapes=[pltpu.VMEM((tm, d), BF16)],
        compiler_params=_params("parallel", "arbitrary"),
        name="norm_matmul",
    )(x, g, w)


def _out_proj_kernel(*refs, n_in):
    x_ref = refs[0]
    o_refs = refs[1:1 + n_in]
    w_refs = refs[1 + n_in:1 + 2 * n_in]
    out_ref = refs[1 + 2 * n_in]
    acc = x_ref[...]
    for o_ref, w_ref in zip(o_refs, w_refs):
        acc = acc + _dot(o_ref[...], w_ref[...])
    out_ref[...] = acc


def _out_proj(x, os_, ws, tm, tn):
    t, d = x.shape
    n_in = len(os_)
    in_specs = [pl.BlockSpec((tm, tn), lambda i, j: (i, j))]
    in_specs += [pl.BlockSpec((tm, o.shape[1]), lambda i, j: (i, 0)) for o in os_]
    in_specs += [pl.BlockSpec((w.shape[0], tn), lambda i, j: (0, j)) for w in ws]
    return pl.pallas_call(
        functools.partial(_out_proj_kernel, n_in=n_in),
        grid=(t // tm, d // tn),
        in_specs=in_specs,
        out_specs=pl.BlockSpec((tm, tn), lambda i, j: (i, j)),
        out_shape=jax.ShapeDtypeStruct((t, d), F32),
        compiler_params=_params("parallel", "arbitrary"),
        name="out_proj",
    )(x, *os_, *ws)


def _mlp_kernel(x_ref, g_ref, w1_ref, w2_ref, o_ref, xn_ref):
    @pl.when(pl.program_id(1) == 0)
    def _():
        x = x_ref[...]
        xn_ref[...] = _rms_rows(x, g_ref[...]).astype(BF16)
        o_ref[...] = x

    h = jnp.maximum(_dot(xn_ref[...], w1_ref[...]), 0.0)
    o_ref[...] += _dot((h * h).astype(BF16), w2_ref[...])


def _mlp(x, g, w1, w2, tm, tf):
    t, d = x.shape
    f = w1.shape[1]
    return pl.pallas_call(
        _mlp_kernel,
        grid=(t // tm, f // tf),
        in_specs=[
            pl.BlockSpec((tm, d), lambda i, j: (i, 0)),
            pl.BlockSpec((1, d), lambda i, j: (0, 0)),
            pl.BlockSpec((d, tf), lambda i, j: (0, j)),
            pl.BlockSpec((tf, d), lambda i, j: (j, 0)),
        ],
        out_specs=pl.BlockSpec((tm, d), lambda i, j: (i, 0)),
        out_shape=jax.ShapeDtypeStruct((t, d), F32),
        scratch_shapes=[pltpu.VMEM((tm, d), BF16)],
        compiler_params=_params("parallel", "arbitrary"),
        name="mlp",
    )(x, g, w1, w2)


def _rope_tables(seq, dim):
    half = dim // 2
    inv = 1.0 / (ROPE_THETA ** (jnp.arange(0, dim, 2, dtype=F32) / dim))
    ang = jnp.arange(seq, dtype=F32)[:, None] * inv[None, :]
    cos, sin = jnp.cos(ang), jnp.sin(ang)
    reps = LANES // dim
    cos_t = jnp.tile(jnp.concatenate([cos, cos], axis=-1), (1, reps))
    sin_t = jnp.tile(jnp.concatenate([-sin, sin], axis=-1), (1, reps))
    return cos_t, sin_t


def _rotate_half(y, dim):
    half = dim // 2
    if dim == LANES:
        return pltpu.roll(y, half, axis=1)
    lane = lax.broadcasted_iota(jnp.int32, y.shape, 1)
    first = (lane % dim) < half
    return jnp.where(first, pltpu.roll(y, LANES - half, axis=1), pltpu.roll(y, half, axis=1))


def _qk64_kernel(p_ref, g_ref, cos_ref, sin_ref, grp_ref, o_ref):
    cos = cos_ref[...]
    sin = sin_ref[...]
    grp = grp_ref[...]
    for t in range(p_ref.shape[2] // LANES):
        cols = slice(t * LANES, (t + 1) * LANES)
        x = p_ref[0, :, cols].astype(F32)
        hi, lo = _split_bf16(x * x)
        ms = (_dot(hi, grp) + _dot(lo, grp)) * (1.0 / A_SUB)
        y = x * lax.rsqrt(ms + NORM_EPS) * g_ref[:, cols]
        o_ref[0, :, cols] = (y * cos + _rotate_half(y, A_SUB) * sin).astype(o_ref.dtype)


def _qk64_prologue(proj, gain, cos_t, sin_t, width, ts, cw):
    b, s, _ = proj.shape
    lane = jnp.arange(LANES)
    grp = (lane[:, None] // A_SUB == lane[None, :] // A_SUB).astype(BF16)
    return pl.pallas_call(
        _qk64_kernel,
        grid=(b, s // ts, width // cw),
        in_specs=[
            pl.BlockSpec((1, ts, cw), lambda bi, i, j: (bi, i, j)),
            pl.BlockSpec((1, cw), lambda bi, i, j: (0, j)),
            pl.BlockSpec((ts, LANES), lambda bi, i, j: (i, 0)),
            pl.BlockSpec((ts, LANES), lambda bi, i, j: (i, 0)),
            pl.BlockSpec((LANES, LANES), lambda bi, i, j: (0, 0)),
        ],
        out_specs=pl.BlockSpec((1, ts, cw), lambda bi, i, j: (bi, i, j)),
        out_shape=jax.ShapeDtypeStruct((b, s, width), BF16),
        compiler_params=_params("parallel", "parallel", "arbitrary"),
        name="qk64_prologue",
    )(proj, gain, cos_t, sin_t, grp)


DIFF_HEADS_PER_STEP = 2


def _diff_attn_kernel(q_ref, k_ref, v_ref, lam_ref, gn_ref, o_ref, q2_ref, m_ref, l_ref, acc_ref,
                      s_a, s_b, p_a, p_b, al_a, al_b, *, seq, t, hp, lam_init):
    heads = range(hp)
    qb = 2 * t
    nq = 2 * qb
    n_strips = nq // LANES
    cols = [slice(h * HEAD_DIM, (h + 1) * HEAD_DIM) for h in heads]
    lv = lam_ref[...]
    lam = (jnp.exp(jnp.sum(lv[0:1] * lv[1:2], axis=1, keepdims=True))
           - jnp.exp(jnp.sum(lv[2:3] * lv[3:4], axis=1, keepdims=True)) + lam_init)
    key_idx = lax.broadcasted_iota(jnp.int32, (t, LANES), 0)
    lane_idx = lax.broadcasted_iota(jnp.int32, (t, LANES), 1)
    lane = lax.broadcasted_iota(jnp.int32, (qb, HEAD_DIM), 1)

    def key_rows(j):
        return pl.ds(pl.multiple_of(j * t, t), t)

    def scores(j, s_ref):
        for h in heads:
            s_ref[h] = _dot_nt(k_ref[0, key_rows(j), cols[h]], q2_ref[h])

    def values(j, p_ref, al_ref):
        for h in heads:
            pv = _dot_tn(v_ref[0, key_rows(j), cols[h]], p_ref[h])
            acc_ref[h] = al_ref[h] * acc_ref[h] + pv

    def softmax(s_ref, p_ref, al_ref, key_off):
        for h in heads:
            for c in range(n_strips):
                strip = slice(c * LANES, (c + 1) * LANES)
                q_lo = (c * LANES) % qb
                if key_off is not None and key_off > q_lo + LANES - 1:
                    p_ref[h, :, strip] = jnp.zeros((t, LANES), BF16)
                    al_ref[h, :, strip] = jnp.ones((1, LANES), F32)
                    continue
                s = s_ref[h, :, strip]
                if key_off is not None and key_off + t - 1 > q_lo:
                    s = jnp.where(key_idx + key_off <= lane_idx + q_lo, s, -jnp.inf)
                m_prev = m_ref[h, :, strip]
                m_new = jnp.maximum(m_prev, jnp.max(s, axis=0, keepdims=True))
                alpha = jnp.exp2(m_prev - m_new)
                p = jnp.exp2(s - m_new)
                l_ref[h, :, strip] = alpha * l_ref[h, :, strip] + jnp.sum(p, axis=0, keepdims=True)
                m_ref[h, :, strip] = m_new
                al_ref[h, :, strip] = alpha
                p_ref[h, :, strip] = p.astype(BF16)

    def q_block(i, carry):
        rows = pl.ds(pl.multiple_of(i * qb, qb), qb)
        for h in heads:
            q = q_ref[0, rows, cols[h]]
            zero = jnp.zeros_like(q)
            q2_ref[h, :qb, :] = jnp.where(lane < A_SUB, q, zero)
            q2_ref[h, qb:, :] = jnp.where(lane < A_SUB, zero, q)
        m_ref[...] = jnp.full(m_ref.shape, -jnp.inf, F32)
        l_ref[...] = jnp.zeros(l_ref.shape, F32)
        acc_ref[...] = jnp.zeros(acc_ref.shape, F32)
        p_b[...] = jnp.zeros(p_b.shape, BF16)
        al_b[...] = jnp.ones(al_b.shape, F32)
        scores(0, s_a)

        def visible_pair(jj, c):
            e = 2 * jj
            values(jnp.maximum(e - 1, 0), p_b, al_b)
            scores(e + 1, s_b)
            softmax(s_a, p_a, al_a, None)
            values(e, p_a, al_a)
            scores(e + 2, s_a)
            softmax(s_b, p_b, al_b, None)
            return c

        lax.fori_loop(0, i, visible_pair, 0)
        e = 2 * i
        values(jnp.maximum(e - 1, 0), p_b, al_b)
        scores(e + 1, s_b)
        softmax(s_a, p_a, al_a, 0)
        values(e, p_a, al_a)
        softmax(s_b, p_b, al_b, t)
        values(e + 1, p_b, al_b)
        for h in heads:
            o12 = acc_ref[h] / l_ref[h]
            o = (o12[:, :qb] - lam * o12[:, qb:]).T
            o_ref[0, rows, cols[h]] = _rms_rows(o, gn_ref[...]).astype(o_ref.dtype)
        return carry

    lax.fori_loop(0, seq // qb, q_block, 0)


def _diff_attention(qk, proj, v_col0, a_lambda, gn, lam_init, n_heads, t):
    b, s, _ = qk.shape
    hp = DIFF_HEADS_PER_STEP
    width = hp * HEAD_DIM
    nq = 4 * t
    kernel = functools.partial(_diff_attn_kernel, seq=s, t=t, hp=hp, lam_init=lam_init)
    k_blk0 = n_heads // hp
    v_blk0 = v_col0 // width
    seq_spec = lambda off: pl.BlockSpec((1, s, width), lambda bi, g: (bi, 0, off + g))
    return pl.pallas_call(
        kernel,
        grid=(b, n_heads // hp),
        in_specs=[
            seq_spec(0), seq_spec(k_blk0), seq_spec(v_blk0),
            pl.BlockSpec((4, A_SUB), lambda bi, g: (0, 0)),
            pl.BlockSpec((1, HEAD_DIM), lambda bi, g: (0, 0)),
        ],
        out_specs=seq_spec(0),
        out_shape=jax.ShapeDtypeStruct((b, s, n_heads * HEAD_DIM), BF16),
        scratch_shapes=[
            pltpu.VMEM((hp, nq, HEAD_DIM), BF16),
            pltpu.VMEM((hp, 1, nq), F32),
            pltpu.VMEM((hp, 1, nq), F32),
            pltpu.VMEM((hp, HEAD_DIM, nq), F32),
            pltpu.VMEM((hp, t, nq), F32), pltpu.VMEM((hp, t, nq), F32),
            pltpu.VMEM((hp, t, nq), BF16), pltpu.VMEM((hp, t, nq), BF16),
            pltpu.VMEM((hp, 1, nq), F32), pltpu.VMEM((hp, 1, nq), F32),
        ],
        compiler_params=_params("parallel", "parallel"),
        name="diff_attention",
    )(qk, qk, proj, a_lambda, gn)


DILATED_UNROLL = 4


def _dilated_kernel(q_ref, k_ref, v_ref, gq_ref, gk_ref, cos_ref, sin_ref, o_ref,
                    qf_ref, kf_ref, vf_ref, ob0, ob1, ob2, ls0, ls1, ls2, *, seq, rows, unroll):
    def prep(c, carry):
        sl = pl.ds(pl.multiple_of(c * rows, rows), rows)
        cos = cos_ref[sl, :]
        sin = sin_ref[sl, :]
        for src, g_ref, dst in ((q_ref, gq_ref, qf_ref), (k_ref, gk_ref, kf_ref)):
            y = _rms_rows(src[0, sl, :].astype(F32), g_ref[...])
            dst[sl, :] = y * cos + _rotate_half(y, HEAD_DIM) * sin
        vf_ref[sl, :] = v_ref[0, sl, :].astype(F32)
        return carry

    lax.fori_loop(0, seq // rows, prep, 0)

    for (window, dil), ob_ref, ls_ref in zip(C_BRANCHES, (ob0, ob1, ob2), (ls0, ls1, ls2)):
        hops = window // dil
        span = dil * hops
        n_blk = seq // span
        run = min(unroll, n_blk)
        segs = unroll // run
        segs_per_class = n_blk // run
        qi = lax.broadcasted_iota(jnp.int32, (hops, 2 * hops), 0)
        kj = lax.broadcasted_iota(jnp.int32, (hops, 2 * hops), 1)
        band = jnp.logical_and(kj >= qi, kj <= qi + hops)
        own_band = (lax.broadcasted_iota(jnp.int32, (hops, hops), 1)
                    <= lax.broadcasted_iota(jnp.int32, (hops, hops), 0))

        def blocks(it, carry, dil=dil, hops=hops, span=span, run=run, segs=segs,
                   segs_per_class=segs_per_class, ob_ref=ob_ref, ls_ref=ls_ref, band=band,
                   own_band=own_band):
            blks = []
            for g in range(segs):
                seg = it * segs + g
                r = seg // segs_per_class
                n0 = (seg - r * segs_per_class) * run
                base = r + n0 * span
                at = lambda u: pl.ds(base + u * span, hops, stride=dil)
                starts_class = segs_per_class == 1
                if starts_class:
                    kb, vb = [None], [None]
                else:
                    before = pl.ds(jnp.maximum(base - span, r), hops, stride=dil)
                    kb = [kf_ref[before, :]]
                    vb = [vf_ref[before, :]]
                kb += [kf_ref[at(u), :] for u in range(run)]
                vb += [vf_ref[at(u), :] for u in range(run)]
                for u in range(run):
                    q = qf_ref[at(u), :].astype(BF16)
                    if kb[u] is None:
                        blks.append((at(u), q, kb[u + 1].astype(BF16), vb[u + 1].astype(BF16),
                                     own_band))
                    else:
                        mask = band if u > 0 else jnp.logical_and(
                            band, jnp.logical_or(kj >= hops, n0 > 0))
                        blks.append((at(u), q,
                                     jnp.concatenate([kb[u], kb[u + 1]], axis=0).astype(BF16),
                                     jnp.concatenate([vb[u], vb[u + 1]], axis=0).astype(BF16), mask))
            s = [jnp.where(mask, _dot_nt(q, k), -jnp.inf) for _, q, k, _, mask in blks]
            m = [jnp.max(sb, axis=1, keepdims=True) for sb in s]
            p = [jnp.exp2(sb - mb) for sb, mb in zip(s, m)]
            den = [jnp.sum(pb, axis=1, keepdims=True) for pb in p]
            o = [_dot(pb.astype(BF16), blk[3]) for pb, blk in zip(p, blks)]
            for blk, ob, mb, db in zip(blks, o, m, den):
                ob_ref[blk[0], :] = ob / db
                ls_ref[blk[0], :] = jnp.broadcast_to(mb + jnp.log2(db), (hops, HEAD_DIM))
            return carry

        lax.fori_loop(0, dil * n_blk // unroll, blocks, 0)

    def merge(c, carry):
        sl = pl.ds(pl.multiple_of(c * rows, rows), rows)
        l0, l1, l2 = ls0[sl, :], ls1[sl, :], ls2[sl, :]
        m = jnp.maximum(jnp.maximum(l0, l1), l2)
        e0, e1, e2 = jnp.exp2(l0 - m), jnp.exp2(l1 - m), jnp.exp2(l2 - m)
        o = (e0 * ob0[sl, :] + e1 * ob1[sl, :] + e2 * ob2[sl, :]) / (e0 + e1 + e2)
        o_ref[0, sl, :] = o.astype(o_ref.dtype)
        return carry

    lax.fori_loop(0, seq // rows, merge, 0)


def _dilated_attention(proj, gq, gk, cos_t, sin_t, n_heads, rows):
    b, s, _ = proj.shape
    seq_spec = lambda off: pl.BlockSpec((1, s, HEAD_DIM), lambda bi, h: (bi, 0, off + h))
    full = lambda shape: pl.BlockSpec(shape, lambda bi, h: (0, 0))
    return pl.pallas_call(
        functools.partial(_dilated_kernel, seq=s, rows=rows, unroll=DILATED_UNROLL),
        grid=(b, n_heads),
        in_specs=[seq_spec(0), seq_spec(n_heads), seq_spec(2 * n_heads),
                  full((1, HEAD_DIM)), full((1, HEAD_DIM)),
                  full((s, HEAD_DIM)), full((s, HEAD_DIM))],
        out_specs=seq_spec(0),
        out_shape=jax.ShapeDtypeStruct((b, s, n_heads * HEAD_DIM), BF16),
        scratch_shapes=[pltpu.VMEM((s, HEAD_DIM), F32) for _ in range(9)],
        compiler_params=_params("parallel", "parallel"),
        name="dilated_attention",
    )(proj, proj, proj, gq, gk, cos_t, sin_t)


HALO = 16


def _conv_kernel(cur_ref, halo_ref, w_ref, o_ref, *, ts, q_scale):
    i = pl.program_id(1)
    j = pl.program_id(2)
    halo = jnp.where(i > 0, halo_ref[0].astype(F32), 0.0)
    xin = jnp.concatenate([halo, cur_ref[0].astype(F32)], axis=0)
    w = w_ref[...]
    y = None
    for tap in range(CONV_K):
        lo = HALO - (CONV_K - 1) + tap
        term = w[tap:tap + 1, :] * xin[lo:lo + ts, :]
        y = term if y is None else y + term
    y = y * jax.nn.sigmoid(y)
    scale = jnp.where(j == 0, q_scale, 1.0)
    for t in range(y.shape[1] // HEAD_DIM):
        cols = slice(t * HEAD_DIM, (t + 1) * HEAD_DIM)
        yt = y[:, cols]
        unit = yt * (lax.rsqrt(jnp.sum(yt * yt, axis=-1, keepdims=True) + NORM_EPS) * scale)
        o_ref[0, :, cols] = jnp.where(j < 2, unit, yt).astype(o_ref.dtype)


def _gdn_prologue(proj, col0, width, conv_w, ts):
    b, s, _ = proj.shape
    blk0 = col0 // width
    rows_per_halo = ts // HALO
    return pl.pallas_call(
        functools.partial(_conv_kernel, ts=ts, q_scale=HEAD_DIM ** -0.5),
        grid=(b, s // ts, 3),
        in_specs=[
            pl.BlockSpec((1, ts, width), lambda bi, i, j: (bi, i, blk0 + j)),
            pl.BlockSpec((1, HALO, width),
                         lambda bi, i, j: (bi, jnp.maximum(i * rows_per_halo - 1, 0), blk0 + j)),
            pl.BlockSpec((CONV_K, width), lambda bi, i, j: (0, j)),
        ],
        out_specs=pl.BlockSpec((1, ts, width), lambda bi, i, j: (bi, i, j)),
        out_shape=jax.ShapeDtypeStruct((b, s, 3 * width), BF16),
        compiler_params=_params("parallel", "parallel", "arbitrary"),
        name="gdn_prologue",
    )(proj, proj, conv_w)


def _dot_x3(a, b):
    (ah, al), (bh, bl) = a, b
    return _dot(ah, bh) + (_dot(ah, bl) + _dot(al, bh))


def _unit_lower_inverses(ms):
    n = ms[0].shape[0]
    row = lax.broadcasted_iota(jnp.int32, (n, n), 0)
    col = lax.broadcasted_iota(jnp.int32, (n, n), 1)
    eye = jnp.where(row == col, 1.0, 0.0)
    invs = [eye - m for m in ms]
    powers = [_split_bf16(m) for m in ms]
    k = 2
    while k < n:
        powers = [_split_bf16(_dot_x3(p, p)) for p in powers]
        invs = [inv + _dot_x3(_split_bf16(inv), p) for inv, p in zip(invs, powers)]
        k *= 2
    return invs


def _gdn_kernel(q_ref, k_ref, v_ref, z_ref, gate_ref, alog_ref, dtb_ref, gn_ref, o_ref, state_ref,
                *, n_heads):
    c = HEAD_DIM
    heads = range(n_heads)

    @pl.when(pl.program_id(1) == 0)
    def _():
        state_ref[...] = jnp.zeros(state_ref.shape, F32)

    gates = gate_ref[0]
    g = -jnp.exp(alog_ref[...]) * jax.nn.softplus(gates + dtb_ref[...])
    beta_all = jax.nn.sigmoid(gates)
    row = lax.broadcasted_iota(jnp.int32, (CHUNK, CHUNK), 0)
    col = lax.broadcasted_iota(jnp.int32, (CHUNK, CHUNK), 1)
    causal = row >= col
    strict = row > col
    tri = jnp.where(causal, 1.0, 0.0).astype(BF16)
    g_hi, g_lo = _split_bf16(g)
    gc = _dot(tri, g_hi) + _dot(tri, g_lo)
    gc_t = jnp.concatenate([gc, jnp.zeros_like(gc)], axis=0).T

    cols = [slice(h * c, (h + 1) * c) for h in heads]
    gcol = [gc[:, h:h + 1] for h in heads]
    bcol = [beta_all[:, n_heads + h:n_heads + h + 1] for h in heads]
    decay_incl = [jnp.exp(jnp.where(causal, gcol[h] - gc_t[h:h + 1, :CHUNK], -jnp.inf)) for h in heads]
    q = [q_ref[0, :, cols[h]] for h in heads]
    k = [k_ref[0, :, cols[h]] for h in heads]
    kf = [k[h].astype(F32) for h in heads]
    kb = [kf[h] * bcol[h] for h in heads]
    m = [_dot_nt(kb[h].astype(BF16), k[h]) * jnp.where(strict, decay_incl[h], 0.0) for h in heads]
    inv = _unit_lower_inverses(m)
    egc = [jnp.exp(gcol[h]) for h in heads]
    rhs = [jnp.concatenate([v_ref[0, :, cols[h]].astype(F32) * bcol[h], kb[h] * egc[h]], axis=1)
           for h in heads]
    sol = [_dot_x3(_split_bf16(inv[h]), _split_bf16(rhs[h])) for h in heads]
    attn = [(_dot_nt(q[h], k[h]) * decay_incl[h]).astype(BF16) for h in heads]
    g_last = [gc[CHUNK - 1:CHUNK, h:h + 1] for h in heads]
    q_dec = [(q[h].astype(F32) * egc[h]).astype(BF16) for h in heads]
    k_dec = [(kf[h] * jnp.exp(g_last[h] - gcol[h])).astype(BF16) for h in heads]
    state = [state_ref[h] for h in heads]
    state_b = [s.astype(BF16) for s in state]
    v_new = [(sol[h][:, :c] - _dot(sol[h][:, c:].astype(BF16), state_b[h])).astype(BF16) for h in heads]
    o = [_dot(q_dec[h], state_b[h]) + _dot(attn[h], v_new[h]) for h in heads]
    for h in heads:
        state_ref[h] = state[h] * jnp.exp(g_last[h]) + _dot_tn(k_dec[h], v_new[h])
    for h in heads:
        z = z_ref[0, :, cols[h]].astype(F32)
        o_ref[0, :, cols[h]] = (_rms_rows(o[h], gn_ref[...]) * (z * jax.nn.sigmoid(z))).astype(o_ref.dtype)


def _gated_delta(qkv, proj, z_col0, gates, a_log, dt_bias, gn, n_heads):
    b, s, _ = qkv.shape
    width = n_heads * HEAD_DIM
    z_blk = z_col0 // width
    chunk_spec = lambda blk: pl.BlockSpec((1, CHUNK, width), lambda bi, ci: (bi, ci, blk))
    row_spec = pl.BlockSpec((1, LANES), lambda bi, ci: (0, 0))
    return pl.pallas_call(
        functools.partial(_gdn_kernel, n_heads=n_heads),
        grid=(b, s // CHUNK),
        in_specs=[chunk_spec(0), chunk_spec(1), chunk_spec(2), chunk_spec(z_blk),
                  pl.BlockSpec((1, CHUNK, LANES), lambda bi, ci: (bi, ci, 0)),
                  row_spec, row_spec, row_spec],
        out_specs=chunk_spec(0),
        out_shape=jax.ShapeDtypeStruct((b, s, width), BF16),
        scratch_shapes=[pltpu.VMEM((n_heads, HEAD_DIM, HEAD_DIM), F32)],
        compiler_params=_params("parallel", "arbitrary"),
        name="gated_delta_rule",
    )(qkv, qkv, qkv, proj, gates, a_log, dt_bias, gn)


TM_PROJ, TN_PROJ = 1024, 512
TM_MLP, TF_MLP = 1024, 512
T_DIFF = 256
TS_ELEM = 512


def _lane_row(v):
    return jnp.zeros((1, LANES), F32).at[0, :v.shape[0]].set(v.astype(F32))


def _diff_delta_layer(xf, b, s, layer_idx, norm, w_in, a_q_norm, a_k_norm, a_lambda, a_sub_norm,
                      b_conv, b_a_log, b_dt_bias, b_out_norm, w_out, cos_a, sin_a):
    t, d = xf.shape
    a_width = d // 2
    b_width = d - a_width
    a_heads = a_width // HEAD_DIM
    b_heads = b_width // HEAD_DIM
    main = 3 * a_width + 4 * b_width
    w_main = w_in[:, :main].astype(BF16)
    w_gate = jnp.zeros((d, LANES), BF16).at[:, :2 * b_heads].set(w_in[:, main:].astype(BF16))
    g = norm[None]
    proj = _norm_matmul(xf, g, w_main, BF16, TM_PROJ, TN_PROJ).reshape(b, s, main)
    gates = _norm_matmul(xf, g, w_gate, F32, TM_PROJ, LANES).reshape(b, s, LANES)

    lam_init = 0.8 - 0.6 * math.exp(-0.3 * layer_idx)
    qk_gain = jnp.concatenate([jnp.tile(a_q_norm * (A_SUB ** -0.5 * math.log2(math.e)), 2 * a_heads),
                               jnp.tile(a_k_norm, 2 * a_heads)])[None]
    qk = _qk64_prologue(proj, qk_gain, cos_a, sin_a, 2 * a_width, TS_ELEM, 512)
    oa = _diff_attention(qk, proj, 2 * a_width, a_lambda, (a_sub_norm * (1.0 - lam_init))[None],
                         lam_init, a_heads, T_DIFF)

    qkv = _gdn_prologue(proj, 3 * a_width, b_width, b_conv, TS_ELEM)
    ob = _gated_delta(qkv, proj, 3 * a_width + 3 * b_width, gates, _lane_row(b_a_log),
                      _lane_row(b_dt_bias), b_out_norm[None], b_heads)

    w_out = w_out.astype(BF16)
    return _out_proj(xf, [oa.reshape(t, a_width), ob.reshape(t, b_width)],
                     [w_out[:a_width], w_out[a_width:]], TM_PROJ, TN_PROJ)


def _dilated_layer(xf, b, s, norm, w_in, q_norm, k_norm, w_out, cos_c, sin_c):
    t, d = xf.shape
    width = w_in.shape[1] // 3
    proj = _norm_matmul(xf, norm[None], w_in.astype(BF16), BF16, TM_PROJ, TN_PROJ)
    q_gain = q_norm * (HEAD_DIM ** -0.5 * math.log2(math.e))
    o = _dilated_attention(proj.reshape(b, s, 3 * width), q_gain[None],
                           k_norm[None], cos_c, sin_c, width // HEAD_DIM, TS_ELEM)
    return _out_proj(xf, [o.reshape(t, width)], [w_out.astype(BF16)], TM_PROJ, TN_PROJ)


def kernel(x, ab_norm, ab_w_in, a_q_norm, a_k_norm, a_lambda, a_sub_norm, b_conv, b_a_log, b_dt_bias,
           b_out_norm, ab_w_out, c_norm, c_w_in, c_q_norm, c_k_norm, c_w_out, mlp_norm, mlp_w1, mlp_w2):
    b, s, d = x.shape
    depth = mlp_w1.shape[0]
    cos_a, sin_a = _rope_tables(s, A_SUB)
    cos_c, sin_c = _rope_tables(s, HEAD_DIM)
    xf = x.reshape(b * s, d)
    for l in range(depth):
        i = l // 2
        if l % 2 == 0:
            xf = _diff_delta_layer(xf, b, s, l, ab_norm[i], ab_w_in[i], a_q_norm[i], a_k_norm[i],
                                   a_lambda[i], a_sub_norm[i], b_conv[i], b_a_log[i], b_dt_bias[i],
                                   b_out_norm[i], ab_w_out[i], cos_a, sin_a)
        else:
            xf = _dilated_layer(xf, b, s, c_norm[i], c_w_in[i], c_q_norm[i], c_k_norm[i], c_w_out[i],
                                cos_c, sin_c)
        xf = _mlp(xf, mlp_norm[l][None], mlp_w1[l].astype(BF16), mlp_w2[l].astype(BF16),
                  TM_MLP, TF_MLP)
    return xf.reshape(b, s, d)
```

```python
import functools
import math

import jax
import jax.numpy as jnp
from jax import lax
from jax.experimental import pallas as pl
from jax.experimental.pallas import tpu as pltpu

F32 = jnp.float32
BF16 = jnp.bfloat16

HEAD_DIM = 128
A_SUB = HEAD_DIM // 2
CONV_K = 4
CHUNK = 64
ROPE_THETA = 10000.0
NORM_EPS = 1e-6
C_BRANCHES = ((128, 1), (512, 4), (2048, 16))
LANES = 128
MXU_COLS = 256
VMEM_LIMIT_BYTES = 56 * 1024 * 1024


def _params(*semantics):
    return pltpu.CompilerParams(dimension_semantics=semantics, vmem_limit_bytes=VMEM_LIMIT_BYTES)


def _rms_rows(x, gain):
    ms = jnp.mean(x * x, axis=-1, keepdims=True)
    return x * lax.rsqrt(ms + NORM_EPS) * gain


def _dot(a, b):
    return jnp.dot(a, b, preferred_element_type=F32)


def _dot_nt(a, b):
    return lax.dot_general(a, b, (((1,), (1,)), ((), ())), preferred_element_type=F32)


def _dot_tn(a, b):
    return lax.dot_general(a, b, (((0,), (0,)), ((), ())), preferred_element_type=F32)


def _split_bf16(x):
    hi = x.astype(BF16)
    lo = (x - hi.astype(F32)).astype(BF16)
    return hi, lo


def _norm_matmul_kernel(x_ref, g_ref, w_ref, o_ref, xn_ref):
    @pl.when(pl.program_id(1) == 0)
    def _():
        xn_ref[...] = _rms_rows(x_ref[...], g_ref[...]).astype(BF16)

    o_ref[...] = _dot(xn_ref[...], w_ref[...]).astype(o_ref.dtype)


def _norm_matmul(x, g, w, out_dtype, tm, tn):
    t, d = x.shape
    n = w.shape[1]
    tn = min(tn, n)
    return pl.pallas_call(
        _norm_matmul_kernel,
        grid=(t // tm, n // tn),
        in_specs=[
            pl.BlockSpec((tm, d), lambda i, j: (i, 0)),
            pl.BlockSpec((1, d), lambda i, j: (0, 0)),
            pl.BlockSpec((d, tn), lambda i, j: (0, j)),
        ],
        out_specs=pl.BlockSpec((tm, tn), lambda i, j: (i, j)),
        out_shape=jax.ShapeDtypeStruct((t, n), out_dtype),
        scratch_shapes=[pltpu.VMEM((tm, d), BF16)],
        compiler_params=_params("parallel", "arbitrary"),
        name="norm_matmul",
    )(x, g, w)


def _out_proj_kernel(*refs, n_in):
    x_ref = refs[0]
    o_refs = refs[1:1 + n_in]
    w_refs = refs[1 + n_in:1 + 2 * n_in]
    out_ref = refs[1 + 2 * n_in]
    acc = x_ref[...]
    for o_ref, w_ref in zip(o_refs, w_refs):
        acc = acc + _dot(o_ref[...], w_ref[...])
    out_ref[...] = acc


def _out_proj(x, os_, ws, tm, tn):
    t, d = x.shape
    n_in = len(os_)
    in_specs = [pl.BlockSpec((tm, tn), lambda i, j: (i, j))]
    in_specs += [pl.BlockSpec((tm, o.shape[1]), lambda i, j: (i, 0)) for o in os_]
    in_specs += [pl.BlockSpec((w.shape[0], tn), lambda i, j: (0, j)) for w in ws]
    return pl.pallas_call(
        functools.partial(_out_proj_kernel, n_in=n_in),
        grid=(t // tm, d // tn),
        in_specs=in_specs,
        out_specs=pl.BlockSpec((tm, tn), lambda i, j: (i, j)),
        out_shape=jax.ShapeDtypeStruct((t, d), F32),
        compiler_params=_params("parallel", "arbitrary"),
        name="out_proj",
    )(x, *os_, *ws)


def _mlp_kernel(x_ref, g_ref, w1_ref, w2_ref, o_ref, xn_ref):
    @pl.when(pl.program_id(1) == 0)
    def _():
        x = x_ref[...]
        xn_ref[...] = _rms_rows(x, g_ref[...]).astype(BF16)
        o_ref[...] = x

    h = jnp.maximum(_dot(xn_ref[...], w1_ref[...]), 0.0)
    o_ref[...] += _dot((h * h).astype(BF16), w2_ref[...])


def _mlp(x, g, w1, w2, tm, tf):
    t, d = x.shape
    f = w1.shape[1]
    return pl.pallas_call(
        _mlp_kernel,
        grid=(t // tm, f // tf),
        in_specs=[
            pl.BlockSpec((tm, d), lambda i, j: (i, 0)),
            pl.BlockSpec((1, d), lambda i, j: (0, 0)),
            pl.BlockSpec((d, tf), lambda i, j: (0, j)),
            pl.BlockSpec((tf, d), lambda i, j: (j, 0)),
        ],
        out_specs=pl.BlockSpec((tm, d), lambda i, j: (i, 0)),
        out_shape=jax.ShapeDtypeStruct((t, d), F32),
        scratch_shapes=[pltpu.VMEM((tm, d), BF16)],
        compiler_params=_params("parallel", "arbitrary"),
        name="mlp",
    )(x, g, w1, w2)


def _rope_tables(seq, dim):
    half = dim // 2
    inv = 1.0 / (ROPE_THETA ** (jnp.arange(0, dim, 2, dtype=F32) / dim))
    ang = jnp.arange(seq, dtype=F32)[:, None] * inv[None, :]
    cos, sin = jnp.cos(ang), jnp.sin(ang)
    reps = LANES // dim
    cos_t = jnp.tile(jnp.concatenate([cos, cos], axis=-1), (1, reps))
    sin_t = jnp.tile(jnp.concatenate([-sin, sin], axis=-1), (1, reps))
    return cos_t, sin_t


def _rotate_half(y, dim):
    half = dim // 2
    if dim == LANES:
        return pltpu.roll(y, half, axis=1)
    lane = lax.broadcasted_iota(jnp.int32, y.shape, 1)
    first = (lane % dim) < half
    return jnp.where(first, pltpu.roll(y, LANES - half, axis=1), pltpu.roll(y, half, axis=1))


def _qk64_kernel(p_ref, g_ref, cos_ref, sin_ref, grp_ref, o_ref):
    cos = cos_ref[...]
    sin = sin_ref[...]
    grp = grp_ref[...]
    for t in range(p_ref.shape[2] // LANES):
        cols = slice(t * LANES, (t + 1) * LANES)
        x = p_ref[0, :, cols].astype(F32)
        hi, lo = _split_bf16(x * x)
        ms = (_dot(hi, grp) + _dot(lo, grp)) * (1.0 / A_SUB)
        y = x * lax.rsqrt(ms + NORM_EPS) * g_ref[:, cols]
        o_ref[0, :, cols] = (y * cos + _rotate_half(y, A_SUB) * sin).astype(o_ref.dtype)


def _qk64_prologue(proj, gain, cos_t, sin_t, width, ts, cw):
    b, s, _ = proj.shape
    lane = jnp.arange(LANES)
    grp = (lane[:, None] // A_SUB == lane[None, :] // A_SUB).astype(BF16)
    return pl.pallas_call(
        _qk64_kernel,
        grid=(b, s // ts, width // cw),
        in_specs=[
            pl.BlockSpec((1, ts, cw), lambda bi, i, j: (bi, i, j)),
            pl.BlockSpec((1, cw), lambda bi, i, j: (0, j)),
            pl.BlockSpec((ts, LANES), lambda bi, i, j: (i, 0)),
            pl.BlockSpec((ts, LANES), lambda bi, i, j: (i, 0)),
            pl.BlockSpec((LANES, LANES), lambda bi, i, j: (0, 0)),
        ],
        out_specs=pl.BlockSpec((1, ts, cw), lambda bi, i, j: (bi, i, j)),
        out_shape=jax.ShapeDtypeStruct((b, s, width), BF16),
        compiler_params=_params("parallel", "parallel", "arbitrary"),
        name="qk64_prologue",
    )(proj, gain, cos_t, sin_t, grp)


DIFF_HEADS_PER_STEP = 2
DEN_ROWS = 16

def _diff_attn_kernel(q_ref, k_ref, v_ref, lam_ref, gn_ref, o_ref, q2_ref, vt_ref, m_ref, acc_ref,
                      s_a, s_b, p_a, p_b, al_a, al_b, *, seq, t, hp, lam_init):
    heads = range(hp)
    qb = 2 * t
    nq = 2 * qb
    cols =[slice(h * HEAD_DIM, (h + 1) * HEAD_DIM) for h in heads]
    lv = lam_ref[...]
    lam = (jnp.exp(jnp.sum(lv[0:1] * lv[1:2], axis=1, keepdims=True))
           - jnp.exp(jnp.sum(lv[2:3] * lv[3:4], axis=1, keepdims=True)) + lam_init)
    key_idx = lax.broadcasted_iota(jnp.int32, (t, LANES), 0)
    lane_idx = lax.broadcasted_iota(jnp.int32, (t, LANES), 1)
    lane = lax.broadcasted_iota(jnp.int32, (qb, HEAD_DIM), 1)

    def key_rows(j):
        return pl.ds(pl.multiple_of(j * t, t), t)

    eye = (lax.broadcasted_iota(jnp.int32, (HEAD_DIM, HEAD_DIM), 0)
           == lax.broadcasted_iota(jnp.int32, (HEAD_DIM, HEAD_DIM), 1)).astype(BF16)

    def transpose_values(j, carry):
        for h in heads:
            vt_ref[h, j, :HEAD_DIM, :] = _dot_nt(eye, v_ref[0, key_rows(j), cols[h]]).astype(BF16)
            vt_ref[h, j, HEAD_DIM:, :] = jnp.ones((DEN_ROWS, t), BF16)
        return carry

    lax.fori_loop(0, seq // t, transpose_values, 0)

    def scores(j, s_ref, h, lanes):
        s_ref[h, :, lanes] = _dot_nt(k_ref[0, key_rows(j), cols[h]], q2_ref[h, lanes, :])

    def values(j, p_ref, al_ref, h, lanes):
        pv = _dot(vt_ref[h, j], p_ref[h, :, lanes])
        acc_ref[h, :, lanes] = al_ref[h, :, lanes] * acc_ref[h, :, lanes] + pv

    def softmax(s_ref, p_ref, al_ref, key_off, h, c):
        strip = slice(c * LANES, (c + 1) * LANES)
        q_lo = (c * LANES) % qb
        if key_off is not None and key_off > q_lo + LANES - 1:
            p_ref[h, :, strip] = jnp.zeros((t, LANES), BF16)
            al_ref[h, :, strip] = jnp.ones((1, LANES), F32)
            return
        s = s_ref[h, :, strip]
        if key_off is not None and key_off + t - 1 > q_lo:
            s = jnp.where(key_idx + key_off <= lane_idx + q_lo, s, -jnp.inf)
        m_prev = m_ref[h, :, strip]
        m_new = jnp.maximum(m_prev, jnp.max(s, axis=0, keepdims=True))
        alpha = jnp.exp2(m_prev - m_new)
        p = jnp.exp2(s - m_new)
        m_ref[h, :, strip] = m_new
        al_ref[h, :, strip] = alpha
        p_ref[h, :, strip] = p.astype(BF16)

    def stage(prev_tile, p_prev, al_prev, next_tile, s_next, s_cur, p_cur, al_cur, key_off):
        for c in range(nq // MXU_COLS):
            lanes = slice(c * MXU_COLS, (c + 1) * MXU_COLS)
            for h in heads:
                values(prev_tile, p_prev, al_prev, h, lanes)
                if next_tile is not None:
                    scores(next_tile, s_next, h, lanes)
                for strip in range(c * MXU_COLS // LANES, (c + 1) * MXU_COLS // LANES):
                    softmax(s_cur, p_cur, al_cur, key_off, h, strip)

    def q_block(i, carry):
        rows = pl.ds(pl.multiple_of(i * qb, qb), qb)
        for h in heads:
            q = q_ref[0, rows, cols[h]]
            zero = jnp.zeros_like(q)
            q2_ref[h, :qb, :] = jnp.where(lane < A_SUB, q, zero)
            q2_ref[h, qb:, :] = jnp.where(lane < A_SUB, zero, q)
        m_ref[...] = jnp.full(m_ref.shape, -jnp.inf, F32)
        acc_ref[...] = jnp.zeros(acc_ref.shape, F32)
        p_b[...] = jnp.zeros(p_b.shape, BF16)
        al_b[...] = jnp.ones(al_b.shape, F32)
        for c in range(nq // MXU_COLS):
            for h in heads:
                scores(0, s_a, h, slice(c * MXU_COLS, (c + 1) * MXU_COLS))

        def visible_pair(jj, c):
            e = 2 * jj
            stage(jnp.maximum(e - 1, 0), p_b, al_b, e + 1, s_b, s_a, p_a, al_a, None)
            stage(e, p_a, al_a, e + 2, s_a, s_b, p_b, al_b, None)
            return c

        lax.fori_loop(0, i, visible_pair, 0)
        e = 2 * i
        stage(jnp.maximum(e - 1, 0), p_b, al_b, e + 1, s_b, s_a, p_a, al_a, 0)
        stage(e, p_a, al_a, None, None, s_b, p_b, al_b, t)
        for c in range(nq // MXU_COLS):
            for h in heads:
                values(e + 1, p_b, al_b, h, slice(c * MXU_COLS, (c + 1) * MXU_COLS))
        for h in heads:
            o12 = acc_ref[h, :HEAD_DIM, :] / acc_ref[h, HEAD_DIM:HEAD_DIM + 1, :]
            o = (o12[:, :qb] - lam * o12[:, qb:]).T
            o_ref[0, rows, cols[h]] = _rms_rows(o, gn_ref[...]).astype(o_ref.dtype)
        return carry

    lax.fori_loop(0, seq // qb, q_block, 0)


def _diff_attention(qk, proj, v_col0, a_lambda, gn, lam_init, n_heads, t):
    b, s, _ = qk.shape
    hp = DIFF_HEADS_PER_STEP
    width = hp * HEAD_DIM
    nq = 4 * t
    kernel = functools.partial(_diff_attn_kernel, seq=s, t=t, hp=hp, lam_init=lam_init)
    k_blk0 = n_heads // hp
    v_blk0 = v_col0 // width
    seq_spec = lambda off: pl.BlockSpec((1, s, width), lambda bi, g: (bi, 0, off + g))
    return pl.pallas_call(
        kernel,
        grid=(b, n_heads // hp),
        in_specs=[
            seq_spec(0), seq_spec(k_blk0), seq_spec(v_blk0),
            pl.BlockSpec((4, A_SUB), lambda bi, g: (0, 0)),
            pl.BlockSpec((1, HEAD_DIM), lambda bi, g: (0, 0)),
        ],
        out_specs=seq_spec(0),
        out_shape=jax.ShapeDtypeStruct((b, s, n_heads * HEAD_DIM), BF16),
        scratch_shapes=[
            pltpu.VMEM((hp, nq, HEAD_DIM), BF16),
            pltpu.VMEM((hp, s // t, HEAD_DIM + DEN_ROWS, t), BF16),
            pltpu.VMEM((hp, 1, nq), F32),
            pltpu.VMEM((hp, HEAD_DIM + DEN_ROWS, nq), F32),
            pltpu.VMEM((hp, t, nq), F32), pltpu.VMEM((hp, t, nq), F32),
            pltpu.VMEM((hp, t, nq), BF16), pltpu.VMEM((hp, t, nq), BF16),
            pltpu.VMEM((hp, 1, nq), F32), pltpu.VMEM((hp, 1, nq), F32),
        ],
        compiler_params=_params("parallel", "parallel"),
        name="diff_attention",
    )(qk, qk, proj, a_lambda, gn)


DILATED_UNROLL = 4


def _dilated_kernel(q_ref, k_ref, v_ref, gq_ref, gk_ref, cos_ref, sin_ref, o_ref,
                    qf_ref, kf_ref, vf_ref, ob0, ob1, ob2, ls0, ls1, ls2, *, seq, rows, unroll):
    def prep(c, carry):
        sl = pl.ds(pl.multiple_of(c * rows, rows), rows)
        cos = cos_ref[sl, :]
        sin = sin_ref[sl, :]
        for src, g_ref, dst in ((q_ref, gq_ref, qf_ref), (k_ref, gk_ref, kf_ref)):
            y = _rms_rows(src[0, sl, :].astype(F32), g_ref[...])
            dst[sl, :] = y * cos + _rotate_half(y, HEAD_DIM) * sin
        vf_ref[sl, :] = v_ref[0, sl, :].astype(F32)
        return carry

    lax.fori_loop(0, seq // rows, prep, 0)

    for (window, dil), ob_ref, ls_ref in zip(C_BRANCHES, (ob0, ob1, ob2), (ls0, ls1, ls2)):
        hops = window // dil
        span = dil * hops
        n_blk = seq // span
        run = min(unroll, n_blk)
        segs = unroll // run
        segs_per_class = n_blk // run
        qi = lax.broadcasted_iota(jnp.int32, (hops, 2 * hops), 0)
        kj = lax.broadcasted_iota(jnp.int32, (hops, 2 * hops), 1)
        band = jnp.logical_and(kj >= qi, kj <= qi + hops)
        own_band = (lax.broadcasted_iota(jnp.int32, (hops, hops), 1)
                    <= lax.broadcasted_iota(jnp.int32, (hops, hops), 0))

        def blocks(it, carry, dil=dil, hops=hops, span=span, run=run, segs=segs,
                   segs_per_class=segs_per_class, ob_ref=ob_ref, ls_ref=ls_ref, band=band,
                   own_band=own_band):
            blks = []
            for g in range(segs):
                seg = it * segs + g
                r = seg // segs_per_class
                n0 = (seg - r * segs_per_class) * run
                base = r + n0 * span
                at = lambda u: pl.ds(base + u * span, hops, stride=dil)
                starts_class = segs_per_class == 1
                if starts_class:
                    kb, vb = [None], [None]
                else:
                    before = pl.ds(jnp.maximum(base - span, r), hops, stride=dil)
                    kb = [kf_ref[before, :]]
                    vb = [vf_ref[before, :]]
                kb += [kf_ref[at(u), :] for u in range(run)]
                vb += [vf_ref[at(u), :] for u in range(run)]
                for u in range(run):
                    q = qf_ref[at(u), :].astype(BF16)
                    if kb[u] is None:
                        blks.append((at(u), q, kb[u + 1].astype(BF16), vb[u + 1].astype(BF16),
                                     own_band))
                    else:
                        mask = band if u > 0 else jnp.logical_and(
                            band, jnp.logical_or(kj >= hops, n0 > 0))
                        blks.append((at(u), q,
                                     jnp.concatenate([kb[u], kb[u + 1]], axis=0).astype(BF16),
                                     jnp.concatenate([vb[u], vb[u + 1]], axis=0).astype(BF16), mask))
            s = [jnp.where(mask, _dot_nt(q, k), -jnp.inf) for _, q, k, _, mask in blks]
            m = [jnp.max(sb, axis=1, keepdims=True) for sb in s]
            p = [jnp.exp2(sb - mb) for sb, mb in zip(s, m)]
            den = [jnp.sum(pb, axis=1, keepdims=True) for pb in p]
            o = [_dot(pb.astype(BF16), blk[3]) for pb, blk in zip(p, blks)]
            for blk, ob, mb, db in zip(blks, o, m, den):
                ob_ref[blk[0], :] = ob / db
                ls_ref[blk[0], :] = jnp.broadcast_to(mb + jnp.log2(db), (hops, HEAD_DIM))
            return carry

        lax.fori_loop(0, dil * n_blk // unroll, blocks, 0)

    def merge(c, carry):
        sl = pl.ds(pl.multiple_of(c * rows, rows), rows)
        l0, l1, l2 = ls0[sl, :], ls1[sl, :], ls2[sl, :]
        m = jnp.maximum(jnp.maximum(l0, l1), l2)
        e0, e1, e2 = jnp.exp2(l0 - m), jnp.exp2(l1 - m), jnp.exp2(l2 - m)
        o = (e0 * ob0[sl, :] + e1 * ob1[sl, :] + e2 * ob2[sl, :]) / (e0 + e1 + e2)
        o_ref[0, sl, :] = o.astype(o_ref.dtype)
        return carry

    lax.fori_loop(0, seq // rows, merge, 0)


def _dilated_attention(proj, gq, gk, cos_t, sin_t, n_heads, rows):
    b, s, _ = proj.shape
    seq_spec = lambda off: pl.BlockSpec((1, s, HEAD_DIM), lambda bi, h: (bi, 0, off + h))
    full = lambda shape: pl.BlockSpec(shape, lambda bi, h: (0, 0))
    return pl.pallas_call(
        functools.partial(_dilated_kernel, seq=s, rows=rows, unroll=DILATED_UNROLL),
        grid=(b, n_heads),
        in_specs=[seq_spec(0), seq_spec(n_heads), seq_spec(2 * n_heads),
                  full((1, HEAD_DIM)), full((1, HEAD_DIM)),
                  full((s, HEAD_DIM)), full((s, HEAD_DIM))],
        out_specs=seq_spec(0),
        out_shape=jax.ShapeDtypeStruct((b, s, n_heads * HEAD_DIM), BF16),
        scratch_shapes=[pltpu.VMEM((s, HEAD_DIM), F32) for _ in range(9)],
        compiler_params=_params("parallel", "parallel"),
        name="dilated_attention",
    )(proj, proj, proj, gq, gk, cos_t, sin_t)


HALO = 16


def _conv_kernel(cur_ref, halo_ref, w_ref, o_ref, *, ts, q_scale):
    i = pl.program_id(1)
    j = pl.program_id(2)
    halo = jnp.where(i > 0, halo_ref[0].astype(F32), 0.0)
    xin = jnp.concatenate([halo, cur_ref[0].astype(F32)], axis=0)
    w = w_ref[...]
    y = None
    for tap in range(CONV_K):
        lo = HALO - (CONV_K - 1) + tap
        term = w[tap:tap + 1, :] * xin[lo:lo + ts, :]
        y = term if y is None else y + term
    y = y * jax.nn.sigmoid(y)
    scale = jnp.where(j == 0, q_scale, 1.0)
    for t in range(y.shape[1] // HEAD_DIM):
        cols = slice(t * HEAD_DIM, (t + 1) * HEAD_DIM)
        yt = y[:, cols]
        unit = yt * (lax.rsqrt(jnp.sum(yt * yt, axis=-1, keepdims=True) + NORM_EPS) * scale)
        o_ref[0, :, cols] = jnp.where(j < 2, unit, yt).astype(o_ref.dtype)


def _gdn_prologue(proj, col0, width, conv_w, ts):
    b, s, _ = proj.shape
    blk0 = col0 // width
    rows_per_halo = ts // HALO
    return pl.pallas_call(
        functools.partial(_conv_kernel, ts=ts, q_scale=HEAD_DIM ** -0.5),
        grid=(b, s // ts, 3),
        in_specs=[
            pl.BlockSpec((1, ts, width), lambda bi, i, j: (bi, i, blk0 + j)),
            pl.BlockSpec((1, HALO, width),
                         lambda bi, i, j: (bi, jnp.maximum(i * rows_per_halo - 1, 0), blk0 + j)),
            pl.BlockSpec((CONV_K, width), lambda bi, i, j: (0, j)),
        ],
        out_specs=pl.BlockSpec((1, ts, width), lambda bi, i, j: (bi, i, j)),
        out_shape=jax.ShapeDtypeStruct((b, s, 3 * width), BF16),
        compiler_params=_params("parallel", "parallel", "arbitrary"),
        name="gdn_prologue",
    )(proj, proj, conv_w)


GDN_CHUNKS_PER_STEP = 2


def _dot_x3(a, b):
    (ah, al), (bh, bl) = a, b
    return _dot(ah, bh) + (_dot(ah, bl) + _dot(al, bh))


def _unit_lower_inverses(ms):
    n = ms[0].shape[0]
    row = lax.broadcasted_iota(jnp.int32, (n, n), 0)
    col = lax.broadcasted_iota(jnp.int32, (n, n), 1)
    eye = jnp.where(row == col, 1.0, 0.0)
    invs = [eye - m for m in ms]
    powers = [_split_bf16(m) for m in ms]
    k = 2
    while k < n:
        powers = [_split_bf16(_dot_x3(p, p)) for p in powers]
        invs = [inv + _dot_x3(_split_bf16(inv), p) for inv, p in zip(invs, powers)]
        k *= 2
    return invs


def _gdn_kernel(q_ref, k_ref, v_ref, z_ref, gate_ref, alog_ref, dtb_ref, gn_ref, o_ref, state_ref,
                *, n_heads, n_chunks):
    c = HEAD_DIM
    heads = range(n_heads)

    @pl.when(pl.program_id(1) == 0)
    def _():
        state_ref[...] = jnp.zeros(state_ref.shape, F32)

    row = lax.broadcasted_iota(jnp.int32, (CHUNK, CHUNK), 0)
    col = lax.broadcasted_iota(jnp.int32, (CHUNK, CHUNK), 1)
    causal = row >= col
    strict = row > col
    tri = jnp.where(causal, 1.0, 0.0).astype(BF16)
    cols = [slice(h * c, (h + 1) * c) for h in heads]

    pairs = [(ci, h) for ci in range(n_chunks) for h in heads]
    rows = [slice(ci * CHUNK, (ci + 1) * CHUNK) for ci in range(n_chunks)]
    gc, gc_t, beta_all = [], [], []
    for ci in range(n_chunks):
        gates = gate_ref[0, rows[ci], :]
        g = -jnp.exp(alog_ref[...]) * jax.nn.softplus(gates + dtb_ref[...])
        beta_all.append(jax.nn.sigmoid(gates))
        g_hi, g_lo = _split_bf16(g)
        gc.append(_dot(tri, g_hi) + _dot(tri, g_lo))
        gc_t.append(jnp.concatenate([gc[ci], jnp.zeros_like(gc[ci])], axis=0).T)
    gcol = [gc[ci][:, h:h + 1] for ci, h in pairs]
    bcol = [beta_all[ci][:, n_heads + h:n_heads + h + 1] for ci, h in pairs]
    decay_incl = [jnp.exp(jnp.where(causal, gcol[x] - gc_t[ci][h:h + 1, :CHUNK], -jnp.inf))
                  for x, (ci, h) in enumerate(pairs)]
    px = range(len(pairs))
    q = [q_ref[0, rows[ci], cols[h]] for ci, h in pairs]
    k = [k_ref[0, rows[ci], cols[h]] for ci, h in pairs]
    kf = [k[x].astype(F32) for x in px]
    kb = [kf[x] * bcol[x] for x in px]
    m = [_dot_nt(kb[x].astype(BF16), k[x]) * jnp.where(strict, decay_incl[x], 0.0) for x in px]
    inv = _unit_lower_inverses(m)
    egc = [jnp.exp(gcol[x]) for x in px]
    rhs = [jnp.concatenate([v_ref[0, rows[ci], cols[h]].astype(F32) * bcol[x], kb[x] * egc[x]], axis=1)
           for x, (ci, h) in enumerate(pairs)]
    sol = [_dot(inv[x].astype(BF16), rhs[x].astype(BF16)) for x in px]
    attn = [(_dot_nt(q[x], k[x]) * decay_incl[x]).astype(BF16) for x in px]
    g_last = [gc[ci][CHUNK - 1:CHUNK, h:h + 1] for ci, h in pairs]
    q_dec = [(q[x].astype(F32) * egc[x]).astype(BF16) for x in px]
    k_dec = [(kf[x] * jnp.exp(g_last[x] - gcol[x])).astype(BF16) for x in px]

    state = [state_ref[h] for h in heads]
    for ci in range(n_chunks):
        xs = [ci * n_heads + h for h in heads]
        state_b = [s.astype(BF16) for s in state]
        v_new = [(sol[x][:, :c] - _dot(sol[x][:, c:].astype(BF16), state_b[h])).astype(BF16)
                 for h, x in enumerate(xs)]
        o = [_dot(q_dec[x], state_b[h]) + _dot(attn[x], v_new[h]) for h, x in enumerate(xs)]
        state = [state[h] * jnp.exp(g_last[x]) + _dot_tn(k_dec[x], v_new[h]) for h, x in enumerate(xs)]
        for h in heads:
            z = z_ref[0, rows[ci], cols[h]].astype(F32)
            o_ref[0, rows[ci], cols[h]] = (_rms_rows(o[h], gn_ref[...])
                                           * (z * jax.nn.sigmoid(z))).astype(o_ref.dtype)
    for h in heads:
        state_ref[h] = state[h]


def _gated_delta(qkv, proj, z_col0, gates, a_log, dt_bias, gn, n_heads):
    b, s, _ = qkv.shape
    width = n_heads * HEAD_DIM
    z_blk = z_col0 // width
    step_rows = GDN_CHUNKS_PER_STEP * CHUNK
    chunk_spec = lambda blk: pl.BlockSpec((1, step_rows, width), lambda bi, ci: (bi, ci, blk))
    row_spec = pl.BlockSpec((1, LANES), lambda bi, ci: (0, 0))
    return pl.pallas_call(
        functools.partial(_gdn_kernel, n_heads=n_heads, n_chunks=GDN_CHUNKS_PER_STEP),
        grid=(b, s // step_rows),
        in_specs=[chunk_spec(0), chunk_spec(1), chunk_spec(2), chunk_spec(z_blk),
                  pl.BlockSpec((1, step_rows, LANES), lambda bi, ci: (bi, ci, 0)),
                  row_spec, row_spec, row_spec],
        out_specs=chunk_spec(0),
        out_shape=jax.ShapeDtypeStruct((b, s, width), BF16),
        scratch_shapes=[pltpu.VMEM((n_heads, HEAD_DIM, HEAD_DIM), F32)],
        compiler_params=_params("parallel", "arbitrary"),
        name="gated_delta_rule",
    )(qkv, qkv, qkv, proj, gates, a_log, dt_bias, gn)


TM_PROJ, TN_PROJ = 1024, 1024
TM_MLP, TF_MLP = 1024, 512
T_DIFF = 256
TS_ELEM = 512


def _lane_row(v):
    return jnp.zeros((1, LANES), F32).at[0, :v.shape[0]].set(v.astype(F32))


def _diff_delta_layer(xf, b, s, layer_idx, norm, w_in, a_q_norm, a_k_norm, a_lambda, a_sub_norm,
                      b_conv, b_a_log, b_dt_bias, b_out_norm, w_out, cos_a, sin_a):
    t, d = xf.shape
    a_width = d // 2
    b_width = d - a_width
    a_heads = a_width // HEAD_DIM
    b_heads = b_width // HEAD_DIM
    main = 3 * a_width + 4 * b_width
    w_main = w_in[:, :main].astype(BF16)
    w_gate = jnp.zeros((d, LANES), BF16).at[:, :2 * b_heads].set(w_in[:, main:].astype(BF16))
    g = norm[None]
    proj = _norm_matmul(xf, g, w_main, BF16, TM_PROJ, TN_PROJ).reshape(b, s, main)
    gates = _norm_matmul(xf, g, w_gate, F32, TM_PROJ, LANES).reshape(b, s, LANES)

    lam_init = 0.8 - 0.6 * math.exp(-0.3 * layer_idx)
    qk_gain = jnp.concatenate([jnp.tile(a_q_norm * (A_SUB ** -0.5 * math.log2(math.e)), 2 * a_heads),
                               jnp.tile(a_k_norm, 2 * a_heads)])[None]
    qk = _qk64_prologue(proj, qk_gain, cos_a, sin_a, 2 * a_width, TS_ELEM, 512)
    oa = _diff_attention(qk, proj, 2 * a_width, a_lambda, (a_sub_norm * (1.0 - lam_init))[None],
                         lam_init, a_heads, T_DIFF)

    qkv = _gdn_prologue(proj, 3 * a_width, b_width, b_conv, TS_ELEM)
    ob = _gated_delta(qkv, proj, 3 * a_width + 3 * b_width, gates, _lane_row(b_a_log),
                      _lane_row(b_dt_bias), b_out_norm[None], b_heads)

    w_out = w_out.astype(BF16)
    return _out_proj(xf, [oa.reshape(t, a_width), ob.reshape(t, b_width)],
                     [w_out[:a_width], w_out[a_width:]], TM_PROJ, TN_PROJ)


def _dilated_layer(xf, b, s, norm, w_in, q_norm, k_norm, w_out, cos_c, sin_c):
    t, d = xf.shape
    width = w_in.shape[1] // 3
    proj = _norm_matmul(xf, norm[None], w_in.astype(BF16), BF16, TM_PROJ, TN_PROJ)
    q_gain = q_norm * (HEAD_DIM ** -0.5 * math.log2(math.e))
    o = _dilated_attention(proj.reshape(b, s, 3 * width), q_gain[None],
                           k_norm[None], cos_c, sin_c, width // HEAD_DIM, TS_ELEM)
    return _out_proj(xf, [o.reshape(t, width)], [w_out.astype(BF16)], TM_PROJ, TN_PROJ)


def kernel(x, ab_norm, ab_w_in, a_q_norm, a_k_norm, a_lambda, a_sub_norm, b_conv, b_a_log, b_dt_bias,
           b_out_norm, ab_w_out, c_norm, c_w_in, c_q_norm, c_k_norm, c_w_out, mlp_norm, mlp_w1, mlp_w2):
    b, s, d = x.shape
    depth = mlp_w1.shape[0]
    cos_a, sin_a = _rope_tables(s, A_SUB)
    cos_c, sin_c = _rope_tables(s, HEAD_DIM)
    xf = x.reshape(b * s, d)
    for l in range(depth):
        i = l // 2
        if l % 2 == 0:
            xf = _diff_delta_layer(xf, b, s, l, ab_norm[i], ab_w_in[i], a_q_norm[i], a_k_norm[i],
                                   a_lambda[i], a_sub_norm[i], b_conv[i], b_a_log[i], b_dt_bias[i],
                                   b_out_norm[i], ab_w_out[i], cos_a, sin_a)
        else:
            xf = _dilated_layer(xf, b, s, c_norm[i], c_w_in[i], c_q_norm[i], c_k_norm[i], c_w_out[i],
                                cos_c, sin_c)
        xf = _mlp(xf, mlp_norm[l][None], mlp_w1[l].astype(BF16), mlp_w2[l].astype(BF16),
                  TM_MLP, TF_MLP)
    return xf.reshape(b, s, d)
```

```python
import functools
import math

import jax
import jax.numpy as jnp
from jax import lax
from jax.experimental import pallas as pl
from jax.experimental.pallas import tpu as pltpu

F32 = jnp.float32
BF16 = jnp.bfloat16

HEAD_DIM = 128
A_SUB = HEAD_DIM // 2
CONV_K = 4
CHUNK = 64
ROPE_THETA = 10000.0
NORM_EPS = 1e-6
C_BRANCHES = ((128, 1), (512, 4), (2048, 16))
LANES = 128
MXU_COLS = 256
VMEM_LIMIT_BYTES = 56 * 1024 * 1024


def _params(*semantics):
    return pltpu.CompilerParams(dimension_semantics=semantics, vmem_limit_bytes=VMEM_LIMIT_BYTES)


def _rms_rows(x, gain):
    ms = jnp.mean(x * x, axis=-1, keepdims=True)
    return x * lax.rsqrt(ms + NORM_EPS) * gain


def _dot(a, b):
    return jnp.dot(a, b, preferred_element_type=F32)


def _dot_nt(a, b):
    return lax.dot_general(a, b, (((1,), (1,)), ((), ())), preferred_element_type=F32)


def _dot_tn(a, b):
    return lax.dot_general(a, b, (((0,), (0,)), ((), ())), preferred_element_type=F32)


def _split_bf16(x):
    hi = x.astype(BF16)
    lo = (x - hi.astype(F32)).astype(BF16)
    return hi, lo


def _norm_matmul_kernel(x_ref, g_ref, w_ref, o_ref, xn_ref):
    @pl.when(pl.program_id(1) == 0)
    def _():
        xn_ref[...] = _rms_rows(x_ref[...], g_ref[...]).astype(BF16)

    o_ref[...] = _dot(xn_ref[...], w_ref[...]).astype(o_ref.dtype)


def _norm_matmul(x, g, w, out_dtype, tm, tn):
    t, d = x.shape
    n = w.shape[1]
    tn = min(tn, n)
    return pl.pallas_call(
        _norm_matmul_kernel,
        grid=(t // tm, n // tn),
        in_specs=[
            pl.BlockSpec((tm, d), lambda i, j: (i, 0)),
            pl.BlockSpec((1, d), lambda i, j: (0, 0)),
            pl.BlockSpec((d, tn), lambda i, j: (0, j)),
        ],
        out_specs=pl.BlockSpec((tm, tn), lambda i, j: (i, j)),
        out_shape=jax.ShapeDtypeStruct((t, n), out_dtype),
        scratch_shapes=[pltpu.VMEM((tm, d), BF16)],
        compiler_params=_params("parallel", "arbitrary"),
        name="norm_matmul",
    )(x, g, w)


def _norm_matmul_heads_kernel(x_ref, g_ref, w_ref, o_ref, xn_ref):
    @pl.when(pl.program_id(1) == 0)
    def _():
        xn_ref[...] = _rms_rows(x_ref[...], g_ref[...]).astype(BF16)

    acc = _dot(xn_ref[...], w_ref[...])
    for h in range(o_ref.shape[1]):
        o_ref[0, h] = acc[:, h * HEAD_DIM:(h + 1) * HEAD_DIM].astype(o_ref.dtype)


def _norm_matmul_heads(x, g, w, batch, tm, tn):
    t, d = x.shape
    n = w.shape[1]
    s = t // batch
    tiles_per_row = s // tm
    hpb = tn // HEAD_DIM
    return pl.pallas_call(
        _norm_matmul_heads_kernel,
        grid=(t // tm, n // tn),
        in_specs=[
            pl.BlockSpec((tm, d), lambda i, j: (i, 0)),
            pl.BlockSpec((1, d), lambda i, j: (0, 0)),
            pl.BlockSpec((d, tn), lambda i, j: (0, j)),
        ],
        out_specs=pl.BlockSpec((1, hpb, tm, HEAD_DIM),
                               lambda i, j: (i // tiles_per_row, j, i % tiles_per_row, 0)),
        out_shape=jax.ShapeDtypeStruct((batch, n // HEAD_DIM, s, HEAD_DIM), BF16),
        scratch_shapes=[pltpu.VMEM((tm, d), BF16)],
        compiler_params=_params("parallel", "arbitrary"),
        name="norm_matmul_heads",
    )(x, g, w)


def _out_proj_kernel(*refs, n_in):
    x_ref = refs[0]
    o_refs = refs[1:1 + n_in]
    w_refs = refs[1 + n_in:1 + 2 * n_in]
    out_ref = refs[1 + 2 * n_in]
    acc = x_ref[...]
    for o_ref, w_ref in zip(o_refs, w_refs):
        acc = acc + _dot(o_ref[...], w_ref[...])
    out_ref[...] = acc


def _out_proj(x, os_, ws, tm, tn):
    t, d = x.shape
    n_in = len(os_)
    in_specs = [pl.BlockSpec((tm, tn), lambda i, j: (i, j))]
    in_specs += [pl.BlockSpec((tm, o.shape[1]), lambda i, j: (i, 0)) for o in os_]
    in_specs += [pl.BlockSpec((w.shape[0], tn), lambda i, j: (0, j)) for w in ws]
    return pl.pallas_call(
        functools.partial(_out_proj_kernel, n_in=n_in),
        grid=(t // tm, d // tn),
        in_specs=in_specs,
        out_specs=pl.BlockSpec((tm, tn), lambda i, j: (i, j)),
        out_shape=jax.ShapeDtypeStruct((t, d), F32),
        compiler_params=_params("parallel", "arbitrary"),
        name="out_proj",
    )(x, *os_, *ws)


def _mlp_kernel(x_ref, g_ref, w1_ref, w2_ref, o_ref, xn_ref):
    @pl.when(pl.program_id(1) == 0)
    def _():
        x = x_ref[...]
        xn_ref[...] = _rms_rows(x, g_ref[...]).astype(BF16)
        o_ref[...] = x

    h = jnp.maximum(_dot(xn_ref[...], w1_ref[...]), 0.0)
    o_ref[...] += _dot((h * h).astype(BF16), w2_ref[...])


def _mlp(x, g, w1, w2, tm, tf):
    t, d = x.shape
    f = w1.shape[1]
    return pl.pallas_call(
        _mlp_kernel,
        grid=(t // tm, f // tf),
        in_specs=[
            pl.BlockSpec((tm, d), lambda i, j: (i, 0)),
            pl.BlockSpec((1, d), lambda i, j: (0, 0)),
            pl.BlockSpec((d, tf), lambda i, j: (0, j)),
            pl.BlockSpec((tf, d), lambda i, j: (j, 0)),
        ],
        out_specs=pl.BlockSpec((tm, d), lambda i, j: (i, 0)),
        out_shape=jax.ShapeDtypeStruct((t, d), F32),
        scratch_shapes=[pltpu.VMEM((tm, d), BF16)],
        compiler_params=_params("parallel", "arbitrary"),
        name="mlp",
    )(x, g, w1, w2)


def _rope_tables(seq, dim):
    half = dim // 2
    inv = 1.0 / (ROPE_THETA ** (jnp.arange(0, dim, 2, dtype=F32) / dim))
    ang = jnp.arange(seq, dtype=F32)[:, None] * inv[None, :]
    cos, sin = jnp.cos(ang), jnp.sin(ang)
    reps = LANES // dim
    cos_t = jnp.tile(jnp.concatenate([cos, cos], axis=-1), (1, reps))
    sin_t = jnp.tile(jnp.concatenate([-sin, sin], axis=-1), (1, reps))
    return cos_t, sin_t


def _rotate_half(y, dim):
    half = dim // 2
    if dim == LANES:
        return pltpu.roll(y, half, axis=1)
    lane = lax.broadcasted_iota(jnp.int32, y.shape, 1)
    first = (lane % dim) < half
    return jnp.where(first, pltpu.roll(y, LANES - half, axis=1), pltpu.roll(y, half, axis=1))


def _qk64_kernel(p_ref, g_ref, cos_ref, sin_ref, grp_ref, o_ref, *, n_groups):
    part = pl.program_id(2) // n_groups

    @pl.when(part < 2)
    def _():
        cos = cos_ref[...]
        sin = sin_ref[...]
        grp = grp_ref[...]
        for t in range(p_ref.shape[2] // LANES):
            cols = slice(t * LANES, (t + 1) * LANES)
            x = p_ref[0, :, cols].astype(F32)
            hi, lo = _split_bf16(x * x)
            ms = (_dot(hi, grp) + _dot(lo, grp)) * (1.0 / A_SUB)
            y = x * lax.rsqrt(ms + NORM_EPS) * g_ref[:, cols]
            o_ref[0, 0, :, cols] = (y * cos + _rotate_half(y, A_SUB) * sin).astype(o_ref.dtype)

    @pl.when(part == 2)
    def _():
        o_ref[0, 0] = p_ref[0]


def _qk64_prologue(proj, gain, cos_t, sin_t, n_heads, ts):
    b, s, _ = proj.shape
    gw = DIFF_HEADS_PER_STEP * HEAD_DIM
    n_groups = n_heads * HEAD_DIM // gw
    lane = jnp.arange(LANES)
    grp = (lane[:, None] // A_SUB == lane[None, :] // A_SUB).astype(BF16)
    return pl.pallas_call(
        functools.partial(_qk64_kernel, n_groups=n_groups),
        grid=(b, s // ts, 3 * n_groups),
        in_specs=[
            pl.BlockSpec((1, ts, gw), lambda bi, i, j: (bi, i, j)),
            pl.BlockSpec((1, gw), lambda bi, i, j: (0, jnp.minimum(j, 2 * n_groups - 1))),
            pl.BlockSpec((ts, LANES), lambda bi, i, j: (i, 0)),
            pl.BlockSpec((ts, LANES), lambda bi, i, j: (i, 0)),
            pl.BlockSpec((LANES, LANES), lambda bi, i, j: (0, 0)),
        ],
        out_specs=pl.BlockSpec((1, 1, ts, gw), lambda bi, i, j: (bi, j % n_groups, i, j // n_groups)),
        out_shape=jax.ShapeDtypeStruct((b, n_groups, s, 3 * gw), BF16),
        compiler_params=_params("parallel", "parallel", "arbitrary"),
        name="qk64_prologue",
    )(proj, gain, cos_t, sin_t, grp)


DIFF_HEADS_PER_STEP = 2
DEN_ROWS = 16

def _diff_attn_kernel(x_ref, lam_ref, gn_ref, o_ref, q2_ref, vt_ref, m_ref, acc_ref,
                      s_a, s_b, p_a, p_b, al_a, al_b, *, seq, t, hp, lam_init):
    heads = range(hp)
    qb = 2 * t
    nq = 2 * qb
    cols = [slice(h * HEAD_DIM, (h + 1) * HEAD_DIM) for h in heads]
    gw = hp * HEAD_DIM
    k_cols = [slice(gw + h * HEAD_DIM, gw + (h + 1) * HEAD_DIM) for h in heads]
    v_cols = [slice(2 * gw + h * HEAD_DIM, 2 * gw + (h + 1) * HEAD_DIM) for h in heads]
    lv = lam_ref[...]
    lam = (jnp.exp(jnp.sum(lv[0:1] * lv[1:2], axis=1, keepdims=True))
           - jnp.exp(jnp.sum(lv[2:3] * lv[3:4], axis=1, keepdims=True)) + lam_init)
    key_idx = lax.broadcasted_iota(jnp.int32, (t, LANES), 0)
    lane_idx = lax.broadcasted_iota(jnp.int32, (t, LANES), 1)
    lane = lax.broadcasted_iota(jnp.int32, (qb, HEAD_DIM), 1)

    def key_rows(j):
        return pl.ds(pl.multiple_of(j * t, t), t)

    eye = (lax.broadcasted_iota(jnp.int32, (HEAD_DIM, HEAD_DIM), 0)
           == lax.broadcasted_iota(jnp.int32, (HEAD_DIM, HEAD_DIM), 1)).astype(BF16)

    def transpose_values(j, carry):
        for h in heads:
            vt_ref[h, j, :HEAD_DIM, :] = _dot_nt(eye, x_ref[0, 0, key_rows(j), v_cols[h]]).astype(BF16)
            vt_ref[h, j, HEAD_DIM:, :] = jnp.ones((DEN_ROWS, t), BF16)
        return carry

    lax.fori_loop(0, seq // t, transpose_values, 0)

    def scores(j, s_ref, h, lanes):
        s_ref[h, :, lanes] = _dot_nt(x_ref[0, 0, key_rows(j), k_cols[h]], q2_ref[h, lanes, :])

    def values(j, p_ref, al_ref, h, lanes):
        pv = _dot(vt_ref[h, j], p_ref[h, :, lanes])
        acc_ref[h, :, lanes] = al_ref[h, :, lanes] * acc_ref[h, :, lanes] + pv

    def softmax(s_ref, p_ref, al_ref, key_off, h, c):
        strip = slice(c * LANES, (c + 1) * LANES)
        q_lo = (c * LANES) % qb
        if key_off is not None and key_off > q_lo + LANES - 1:
            p_ref[h, :, strip] = jnp.zeros((t, LANES), BF16)
            al_ref[h, :, strip] = jnp.ones((1, LANES), F32)
            return
        s = s_ref[h, :, strip]
        if key_off is not None and key_off + t - 1 > q_lo:
            s = jnp.where(key_idx + key_off <= lane_idx + q_lo, s, -jnp.inf)
        m_prev = m_ref[h, :, strip]
        m_new = jnp.maximum(m_prev, jnp.max(s, axis=0, keepdims=True))
        alpha = jnp.exp2(m_prev - m_new)
        p = jnp.exp2(s - m_new)
        m_ref[h, :, strip] = m_new
        al_ref[h, :, strip] = alpha
        p_ref[h, :, strip] = p.astype(BF16)

    def stage(prev_tile, p_prev, al_prev, next_tile, s_next, s_cur, p_cur, al_cur, key_off):
        for c in range(nq // MXU_COLS):
            lanes = slice(c * MXU_COLS, (c + 1) * MXU_COLS)
            for h in heads:
                values(prev_tile, p_prev, al_prev, h, lanes)
                if next_tile is not None:
                    scores(next_tile, s_next, h, lanes)
                for strip in range(c * MXU_COLS // LANES, (c + 1) * MXU_COLS // LANES):
                    softmax(s_cur, p_cur, al_cur, key_off, h, strip)

    def q_block(i, carry):
        rows = pl.ds(pl.multiple_of(i * qb, qb), qb)
        for h in heads:
            q = x_ref[0, 0, rows, cols[h]]
            zero = jnp.zeros_like(q)
            q2_ref[h, :qb, :] = jnp.where(lane < A_SUB, q, zero)
            q2_ref[h, qb:, :] = jnp.where(lane < A_SUB, zero, q)
        m_ref[...] = jnp.full(m_ref.shape, -jnp.inf, F32)
        acc_ref[...] = jnp.zeros(acc_ref.shape, F32)
        p_b[...] = jnp.zeros(p_b.shape, BF16)
        al_b[...] = jnp.ones(al_b.shape, F32)
        for c in range(nq // MXU_COLS):
            for h in heads:
                scores(0, s_a, h, slice(c * MXU_COLS, (c + 1) * MXU_COLS))

        def visible_pair(jj, c):
            e = 2 * jj
            stage(jnp.maximum(e - 1, 0), p_b, al_b, e + 1, s_b, s_a, p_a, al_a, None)
            stage(e, p_a, al_a, e + 2, s_a, s_b, p_b, al_b, None)
            return c

        lax.fori_loop(0, i, visible_pair, 0)
        e = 2 * i
        stage(jnp.maximum(e - 1, 0), p_b, al_b, e + 1, s_b, s_a, p_a, al_a, 0)
        stage(e, p_a, al_a, None, None, s_b, p_b, al_b, t)
        for c in range(nq // MXU_COLS):
            for h in heads:
                values(e + 1, p_b, al_b, h, slice(c * MXU_COLS, (c + 1) * MXU_COLS))
        for h in heads:
            o12 = acc_ref[h, :HEAD_DIM, :] / acc_ref[h, HEAD_DIM:HEAD_DIM + 1, :]
            o = (o12[:, :qb] - lam * o12[:, qb:]).T
            o_ref[0, rows, cols[h]] = _rms_rows(o, gn_ref[...]).astype(o_ref.dtype)
        return carry

    lax.fori_loop(0, seq // qb, q_block, 0)


def _diff_attention(qkv, a_lambda, gn, lam_init, t):
    b, n_groups, s, gw3 = qkv.shape
    hp = DIFF_HEADS_PER_STEP
    width = hp * HEAD_DIM
    nq = 4 * t
    kernel = functools.partial(_diff_attn_kernel, seq=s, t=t, hp=hp, lam_init=lam_init)
    return pl.pallas_call(
        kernel,
        grid=(b, n_groups),
        in_specs=[
            pl.BlockSpec((1, 1, s, gw3), lambda bi, g: (bi, g, 0, 0)),
            pl.BlockSpec((4, A_SUB), lambda bi, g: (0, 0)),
            pl.BlockSpec((1, HEAD_DIM), lambda bi, g: (0, 0)),
        ],
        out_specs=pl.BlockSpec((1, s, width), lambda bi, g: (bi, 0, g)),
        out_shape=jax.ShapeDtypeStruct((b, s, n_groups * width), BF16),
        scratch_shapes=[
            pltpu.VMEM((hp, nq, HEAD_DIM), BF16),
            pltpu.VMEM((hp, s // t, HEAD_DIM + DEN_ROWS, t), BF16),
            pltpu.VMEM((hp, 1, nq), F32),
            pltpu.VMEM((hp, HEAD_DIM + DEN_ROWS, nq), F32),
            pltpu.VMEM((hp, t, nq), F32), pltpu.VMEM((hp, t, nq), F32),
            pltpu.VMEM((hp, t, nq), BF16), pltpu.VMEM((hp, t, nq), BF16),
            pltpu.VMEM((hp, 1, nq), F32), pltpu.VMEM((hp, 1, nq), F32),
        ],
        compiler_params=_params("parallel", "parallel"),
        name="diff_attention",
    )(qkv, a_lambda, gn)


DILATED_UNROLL = 4


def _dilated_kernel(q_ref, k_ref, v_ref, gq_ref, gk_ref, cos_ref, sin_ref, o_ref,
                    qf_ref, kf_ref, vf_ref, ob0, ob1, ob2, ls0, ls1, ls2, *, seq, rows, unroll):
    def prep(c, carry):
        sl = pl.ds(pl.multiple_of(c * rows, rows), rows)
        cos = cos_ref[sl, :]
        sin = sin_ref[sl, :]
        for src, g_ref, dst in ((q_ref, gq_ref, qf_ref), (k_ref, gk_ref, kf_ref)):
            y = _rms_rows(src[0, 0, sl, :].astype(F32), g_ref[...])
            dst[sl, :] = y * cos + _rotate_half(y, HEAD_DIM) * sin
        vf_ref[sl, :] = v_ref[0, 0, sl, :].astype(F32)
        return carry

    lax.fori_loop(0, seq // rows, prep, 0)

    for (window, dil), ob_ref, ls_ref in zip(C_BRANCHES, (ob0, ob1, ob2), (ls0, ls1, ls2)):
        hops = window // dil
        span = dil * hops
        n_blk = seq // span
        run = min(unroll, n_blk)
        segs = unroll // run
        segs_per_class = n_blk // run
        qi = lax.broadcasted_iota(jnp.int32, (hops, 2 * hops), 0)
        kj = lax.broadcasted_iota(jnp.int32, (hops, 2 * hops), 1)
        band = jnp.logical_and(kj >= qi, kj <= qi + hops)
        own_band = (lax.broadcasted_iota(jnp.int32, (hops, hops), 1)
                    <= lax.broadcasted_iota(jnp.int32, (hops, hops), 0))

        def blocks(it, carry, dil=dil, hops=hops, span=span, run=run, segs=segs,
                   segs_per_class=segs_per_class, ob_ref=ob_ref, ls_ref=ls_ref, band=band,
                   own_band=own_band):
            blks = []
            for g in range(segs):
                seg = it * segs + g
                r = seg // segs_per_class
                n0 = (seg - r * segs_per_class) * run
                base = r + n0 * span
                at = lambda u: pl.ds(base + u * span, hops, stride=dil)
                starts_class = segs_per_class == 1
                if starts_class:
                    kb, vb = [None], [None]
                else:
                    before = pl.ds(jnp.maximum(base - span, r), hops, stride=dil)
                    kb = [kf_ref[before, :]]
                    vb = [vf_ref[before, :]]
                kb += [kf_ref[at(u), :] for u in range(run)]
                vb += [vf_ref[at(u), :] for u in range(run)]
                for u in range(run):
                    q = qf_ref[at(u), :].astype(BF16)
                    if kb[u] is None:
                        blks.append((at(u), q, kb[u + 1].astype(BF16), vb[u + 1].astype(BF16),
                                     own_band))
                    else:
                        mask = band if u > 0 else jnp.logical_and(
                            band, jnp.logical_or(kj >= hops, n0 > 0))
                        blks.append((at(u), q,
                                     jnp.concatenate([kb[u], kb[u + 1]], axis=0).astype(BF16),
                                     jnp.concatenate([vb[u], vb[u + 1]], axis=0).astype(BF16), mask))
            s = [jnp.where(mask, _dot_nt(q, k), -jnp.inf) for _, q, k, _, mask in blks]
            m = [jnp.max(sb, axis=1, keepdims=True) for sb in s]
            p = [jnp.exp2(sb - mb) for sb, mb in zip(s, m)]
            den = [jnp.sum(pb, axis=1, keepdims=True) for pb in p]
            o = [_dot(pb.astype(BF16), blk[3]) for pb, blk in zip(p, blks)]
            for blk, ob, mb, db in zip(blks, o, m, den):
                ob_ref[blk[0], :] = ob / db
                ls_ref[blk[0], :] = jnp.broadcast_to(mb + jnp.log2(db), (hops, HEAD_DIM))
            return carry

        lax.fori_loop(0, dil * n_blk // unroll, blocks, 0)

    def merge(c, carry):
        sl = pl.ds(pl.multiple_of(c * rows, rows), rows)
        l0, l1, l2 = ls0[sl, :], ls1[sl, :], ls2[sl, :]
        m = jnp.maximum(jnp.maximum(l0, l1), l2)
        e0, e1, e2 = jnp.exp2(l0 - m), jnp.exp2(l1 - m), jnp.exp2(l2 - m)
        o = (e0 * ob0[sl, :] + e1 * ob1[sl, :] + e2 * ob2[sl, :]) / (e0 + e1 + e2)
        o_ref[0, sl, :] = o.astype(o_ref.dtype)
        return carry

    lax.fori_loop(0, seq // rows, merge, 0)


def _dilated_attention(proj, gq, gk, cos_t, sin_t, n_heads, rows):
    b, _, s, _ = proj.shape
    head_spec = lambda off: pl.BlockSpec((1, 1, s, HEAD_DIM), lambda bi, h: (bi, off + h, 0, 0))
    full = lambda shape: pl.BlockSpec(shape, lambda bi, h: (0, 0))
    return pl.pallas_call(
        functools.partial(_dilated_kernel, seq=s, rows=rows, unroll=DILATED_UNROLL),
        grid=(b, n_heads),
        in_specs=[head_spec(0), head_spec(n_heads), head_spec(2 * n_heads),
                  full((1, HEAD_DIM)), full((1, HEAD_DIM)),
                  full((s, HEAD_DIM)), full((s, HEAD_DIM))],
        out_specs=pl.BlockSpec((1, s, HEAD_DIM), lambda bi, h: (bi, 0, h)),
        out_shape=jax.ShapeDtypeStruct((b, s, n_heads * HEAD_DIM), BF16),
        scratch_shapes=[pltpu.VMEM((s, HEAD_DIM), F32) for _ in range(9)],
        compiler_params=_params("parallel", "parallel"),
        name="dilated_attention",
    )(proj, proj, proj, gq, gk, cos_t, sin_t)


HALO = 16


def _conv_kernel(cur_ref, halo_ref, w_ref, o_ref, *, ts, q_scale):
    i = pl.program_id(1)
    j = pl.program_id(2)
    halo = jnp.where(i > 0, halo_ref[0].astype(F32), 0.0)
    xin = jnp.concatenate([halo, cur_ref[0].astype(F32)], axis=0)
    w = w_ref[...]
    y = None
    for tap in range(CONV_K):
        lo = HALO - (CONV_K - 1) + tap
        term = w[tap:tap + 1, :] * xin[lo:lo + ts, :]
        y = term if y is None else y + term
    y = y * jax.nn.sigmoid(y)
    scale = jnp.where(j == 0, q_scale, 1.0)
    for t in range(y.shape[1] // HEAD_DIM):
        cols = slice(t * HEAD_DIM, (t + 1) * HEAD_DIM)
        yt = y[:, cols]
        unit = yt * (lax.rsqrt(jnp.sum(yt * yt, axis=-1, keepdims=True) + NORM_EPS) * scale)
        o_ref[0, :, cols] = jnp.where(j < 2, unit, yt).astype(o_ref.dtype)


def _gdn_prologue(proj, col0, width, conv_w, ts):
    b, s, _ = proj.shape
    blk0 = col0 // width
    rows_per_halo = ts // HALO
    return pl.pallas_call(
        functools.partial(_conv_kernel, ts=ts, q_scale=HEAD_DIM ** -0.5),
        grid=(b, s // ts, 3),
        in_specs=[
            pl.BlockSpec((1, ts, width), lambda bi, i, j: (bi, i, blk0 + j)),
            pl.BlockSpec((1, HALO, width),
                         lambda bi, i, j: (bi, jnp.maximum(i * rows_per_halo - 1, 0), blk0 + j)),
            pl.BlockSpec((CONV_K, width), lambda bi, i, j: (0, j)),
        ],
        out_specs=pl.BlockSpec((1, ts, width), lambda bi, i, j: (bi, i, j)),
        out_shape=jax.ShapeDtypeStruct((b, s, 3 * width), BF16),
        compiler_params=_params("parallel", "parallel", "arbitrary"),
        name="gdn_prologue",
    )(proj, proj, conv_w)


GDN_CHUNKS_PER_STEP = 2


def _dot_x3(a, b):
    (ah, al), (bh, bl) = a, b
    return _dot(ah, bh) + (_dot(ah, bl) + _dot(al, bh))


def _unit_lower_inverses(ms):
    n = ms[0].shape[0]
    row = lax.broadcasted_iota(jnp.int32, (n, n), 0)
    col = lax.broadcasted_iota(jnp.int32, (n, n), 1)
    eye = jnp.where(row == col, 1.0, 0.0)
    invs = [eye - m for m in ms]
    powers = [_split_bf16(m) for m in ms]
    k = 2
    while k < n:
        powers = [_split_bf16(_dot_x3(p, p)) for p in powers]
        invs = [inv + _dot_x3(_split_bf16(inv), p) for inv, p in zip(invs, powers)]
        k *= 2
    return invs


def _gdn_kernel(q_ref, k_ref, v_ref, z_ref, gate_ref, alog_ref, dtb_ref, gn_ref, o_ref, state_ref,
                *, n_heads, n_chunks):
    c = HEAD_DIM
    heads = range(n_heads)

    @pl.when(pl.program_id(1) == 0)
    def _():
        state_ref[...] = jnp.zeros(state_ref.shape, F32)

    row = lax.broadcasted_iota(jnp.int32, (CHUNK, CHUNK), 0)
    col = lax.broadcasted_iota(jnp.int32, (CHUNK, CHUNK), 1)
    causal = row >= col
    strict = row > col
    tri = jnp.where(causal, 1.0, 0.0).astype(BF16)
    cols = [slice(h * c, (h + 1) * c) for h in heads]

    pairs = [(ci, h) for ci in range(n_chunks) for h in heads]
    rows = [slice(ci * CHUNK, (ci + 1) * CHUNK) for ci in range(n_chunks)]
    gc, gc_t, beta_all = [], [], []
    for ci in range(n_chunks):
        gates = gate_ref[0, rows[ci], :]
        g = -jnp.exp(alog_ref[...]) * jax.nn.softplus(gates + dtb_ref[...])
        beta_all.append(jax.nn.sigmoid(gates))
        g_hi, g_lo = _split_bf16(g)
        gc.append(_dot(tri, g_hi) + _dot(tri, g_lo))
        gc_t.append(jnp.concatenate([gc[ci], jnp.zeros_like(gc[ci])], axis=0).T)
    gcol = [gc[ci][:, h:h + 1] for ci, h in pairs]
    bcol = [beta_all[ci][:, n_heads + h:n_heads + h + 1] for ci, h in pairs]
    decay_incl = [jnp.exp(jnp.where(causal, gcol[x] - gc_t[ci][h:h + 1, :CHUNK], -jnp.inf))
                  for x, (ci, h) in enumerate(pairs)]
    px = range(len(pairs))
    q = [q_ref[0, rows[ci], cols[h]] for ci, h in pairs]
    k = [k_ref[0, rows[ci], cols[h]] for ci, h in pairs]
    kf = [k[x].astype(F32) for x in px]
    kb = [kf[x] * bcol[x] for x in px]
    m = [_dot_nt(kb[x].astype(BF16), k[x]) * jnp.where(strict, decay_incl[x], 0.0) for x in px]
    inv = _unit_lower_inverses(m)
    egc = [jnp.exp(gcol[x]) for x in px]
    rhs = [jnp.concatenate([v_ref[0, rows[ci], cols[h]].astype(F32) * bcol[x], kb[x] * egc[x]], axis=1)
           for x, (ci, h) in enumerate(pairs)]
    sol = [_dot(inv[x].astype(BF16), rhs[x].astype(BF16)) for x in px]
    attn = [(_dot_nt(q[x], k[x]) * decay_incl[x]).astype(BF16) for x in px]
    g_last = [gc[ci][CHUNK - 1:CHUNK, h:h + 1] for ci, h in pairs]
    q_dec = [(q[x].astype(F32) * egc[x]).astype(BF16) for x in px]
    k_dec = [(kf[x] * jnp.exp(g_last[x] - gcol[x])).astype(BF16) for x in px]

    state = [state_ref[h] for h in heads]
    for ci in range(n_chunks):
        xs = [ci * n_heads + h for h in heads]
        state_b = [s.astype(BF16) for s in state]
        v_new = [(sol[x][:, :c] - _dot(sol[x][:, c:].astype(BF16), state_b[h])).astype(BF16)
                 for h, x in enumerate(xs)]
        o = [_dot(q_dec[x], state_b[h]) + _dot(attn[x], v_new[h]) for h, x in enumerate(xs)]
        state = [state[h] * jnp.exp(g_last[x]) + _dot_tn(k_dec[x], v_new[h]) for h, x in enumerate(xs)]
        for h in heads:
            z = z_ref[0, rows[ci], cols[h]].astype(F32)
            o_ref[0, rows[ci], cols[h]] = (_rms_rows(o[h], gn_ref[...])
                                           * (z * jax.nn.sigmoid(z))).astype(o_ref.dtype)
    for h in heads:
        state_ref[h] = state[h]


def _gated_delta(qkv, proj, z_col0, gates, a_log, dt_bias, gn, n_heads):
    b, s, _ = qkv.shape
    width = n_heads * HEAD_DIM
    z_blk = z_col0 // width
    step_rows = GDN_CHUNKS_PER_STEP * CHUNK
    chunk_spec = lambda blk: pl.BlockSpec((1, step_rows, width), lambda bi, ci: (bi, ci, blk))
    row_spec = pl.BlockSpec((1, LANES), lambda bi, ci: (0, 0))
    return pl.pallas_call(
        functools.partial(_gdn_kernel, n_heads=n_heads, n_chunks=GDN_CHUNKS_PER_STEP),
        grid=(b, s // step_rows),
        in_specs=[chunk_spec(0), chunk_spec(1), chunk_spec(2), chunk_spec(z_blk),
                  pl.BlockSpec((1, step_rows, LANES), lambda bi, ci: (bi, ci, 0)),
                  row_spec, row_spec, row_spec],
        out_specs=chunk_spec(0),
        out_shape=jax.ShapeDtypeStruct((b, s, width), BF16),
        scratch_shapes=[pltpu.VMEM((n_heads, HEAD_DIM, HEAD_DIM), F32)],
        compiler_params=_params("parallel", "arbitrary"),
        name="gated_delta_rule",
    )(qkv, qkv, qkv, proj, gates, a_log, dt_bias, gn)


TM_PROJ, TN_PROJ = 1024, 1024
TM_MLP, TF_MLP = 1024, 512
T_DIFF = 256
TS_ELEM = 512


def _lane_row(v):
    return jnp.zeros((1, LANES), F32).at[0, :v.shape[0]].set(v.astype(F32))


def _diff_delta_layer(xf, b, s, layer_idx, norm, w_in, a_q_norm, a_k_norm, a_lambda, a_sub_norm,
                      b_conv, b_a_log, b_dt_bias, b_out_norm, w_out, cos_a, sin_a):
    t, d = xf.shape
    a_width = d // 2
    b_width = d - a_width
    a_heads = a_width // HEAD_DIM
    b_heads = b_width // HEAD_DIM
    main = 3 * a_width + 4 * b_width
    w_main = w_in[:, :main].astype(BF16)
    w_gate = jnp.zeros((d, LANES), BF16).at[:, :2 * b_heads].set(w_in[:, main:].astype(BF16))
    g = norm[None]
    proj = _norm_matmul(xf, g, w_main, BF16, TM_PROJ, TN_PROJ).reshape(b, s, main)
    gates = _norm_matmul(xf, g, w_gate, F32, TM_PROJ, LANES).reshape(b, s, LANES)

    lam_init = 0.8 - 0.6 * math.exp(-0.3 * layer_idx)
    qk_gain = jnp.concatenate([jnp.tile(a_q_norm * (A_SUB ** -0.5 * math.log2(math.e)), 2 * a_heads),
                               jnp.tile(a_k_norm, 2 * a_heads)])[None]
    a_qkv = _qk64_prologue(proj, qk_gain, cos_a, sin_a, a_heads, TS_ELEM)
    oa = _diff_attention(a_qkv, a_lambda, (a_sub_norm * (1.0 - lam_init))[None], lam_init, T_DIFF)

    qkv = _gdn_prologue(proj, 3 * a_width, b_width, b_conv, TS_ELEM)
    ob = _gated_delta(qkv, proj, 3 * a_width + 3 * b_width, gates, _lane_row(b_a_log),
                      _lane_row(b_dt_bias), b_out_norm[None], b_heads)

    w_out = w_out.astype(BF16)
    return _out_proj(xf, [oa.reshape(t, a_width), ob.reshape(t, b_width)],
                     [w_out[:a_width], w_out[a_width:]], TM_PROJ, TN_PROJ)


def _dilated_layer(xf, b, s, norm, w_in, q_norm, k_norm, w_out, cos_c, sin_c):
    t, d = xf.shape
    width = w_in.shape[1] // 3
    proj = _norm_matmul_heads(xf, norm[None], w_in.astype(BF16), b, TM_PROJ, TN_PROJ)
    q_gain = q_norm * (HEAD_DIM ** -0.5 * math.log2(math.e))
    o = _dilated_attention(proj, q_gain[None], k_norm[None], cos_c, sin_c, width // HEAD_DIM,
                           TS_ELEM)
    return _out_proj(xf, [o.reshape(t, width)], [w_out.astype(BF16)], TM_PROJ, TN_PROJ)


def kernel(x, ab_norm, ab_w_in, a_q_norm, a_k_norm, a_lambda, a_sub_norm, b_conv, b_a_log, b_dt_bias,
           b_out_norm, ab_w_out, c_norm, c_w_in, c_q_norm, c_k_norm, c_w_out, mlp_norm, mlp_w1, mlp_w2):
    b, s, d = x.shape
    depth = mlp_w1.shape[0]
    cos_a, sin_a = _rope_tables(s, A_SUB)
    cos_c, sin_c = _rope_tables(s, HEAD_DIM)
    xf = x.reshape(b * s, d)
    for l in range(depth):
        i = l // 2
        if l % 2 == 0:
            xf = _diff_delta_layer(xf, b, s, l, ab_norm[i], ab_w_in[i], a_q_norm[i], a_k_norm[i],
                                   a_lambda[i], a_sub_norm[i], b_conv[i], b_a_log[i], b_dt_bias[i],
                                   b_out_norm[i], ab_w_out[i], cos_a, sin_a)
        else:
            xf = _dilated_layer(xf, b, s, c_norm[i], c_w_in[i], c_q_norm[i], c_k_norm[i], c_w_out[i],
                                cos_c, sin_c)
        xf = _mlp(xf, mlp_norm[l][None], mlp_w1[l].astype(BF16), mlp_w2[l].astype(BF16),
                  TM_MLP, TF_MLP)
    return xf.reshape(b, s, d)
```

```python
import functools
import math

import jax
import jax.numpy as jnp
from jax import lax
from jax.experimental import pallas as pl
from jax.experimental.pallas import tpu as pltpu

F32 = jnp.float32
BF16 = jnp.bfloat16

HEAD_DIM = 128
A_SUB = HEAD_DIM // 2
CONV_K = 4
CHUNK = 64
ROPE_THETA = 10000.0
NORM_EPS = 1e-6
C_BRANCHES = ((128, 1), (512, 4), (2048, 16))
LANES = 128
MXU_COLS = 256
VMEM_LIMIT_BYTES = 56 * 1024 * 1024


def _params(*semantics):
    return pltpu.CompilerParams(dimension_semantics=semantics, vmem_limit_bytes=VMEM_LIMIT_BYTES)


def _rms_rows(x, gain):
    ms = jnp.mean(x * x, axis=-1, keepdims=True)
    return x * lax.rsqrt(ms + NORM_EPS) * gain


def _dot(a, b):
    return jnp.dot(a, b, preferred_element_type=F32)


def _dot_nt(a, b):
    return lax.dot_general(a, b, (((1,), (1,)), ((), ())), preferred_element_type=F32)


def _dot_tn(a, b):
    return lax.dot_general(a, b, (((0,), (0,)), ((), ())), preferred_element_type=F32)


def _split_bf16(x):
    hi = x.astype(BF16)
    lo = (x - hi.astype(F32)).astype(BF16)
    return hi, lo


def _norm_matmul_kernel(x_ref, g_ref, w_ref, o_ref, xn_ref):
    @pl.when(pl.program_id(1) == 0)
    def _():
        xn_ref[...] = _rms_rows(x_ref[...], g_ref[...]).astype(BF16)

    o_ref[...] = _dot(xn_ref[...], w_ref[...]).astype(o_ref.dtype)


def _layer_spec(rows, cols, layer, index_map):
    return pl.BlockSpec((None, rows, cols), lambda i, j: (layer,) + index_map(i, j))


def _norm_matmul(x, g, w, layer, n, out_dtype, tm, tn):
    t, d = x.shape
    tn = min(tn, n)
    return pl.pallas_call(
        _norm_matmul_kernel,
        grid=(t // tm, n // tn),
        in_specs=[
            pl.BlockSpec((tm, d), lambda i, j: (i, 0)),
            pl.BlockSpec((1, d), lambda i, j: (0, 0)),
            _layer_spec(d, tn, layer, lambda i, j: (0, j)),
        ],
        out_specs=pl.BlockSpec((tm, tn), lambda i, j: (i, j)),
        out_shape=jax.ShapeDtypeStruct((t, n), out_dtype),
        scratch_shapes=[pltpu.VMEM((tm, d), BF16)],
        compiler_params=_params("parallel", "arbitrary"),
        name="norm_matmul",
    )(x, g, w)


def _out_proj_kernel(*refs, n_in):
    x_ref = refs[0]
    o_refs = refs[1:1 + n_in]
    w_refs = refs[1 + n_in:1 + 2 * n_in]
    out_ref = refs[1 + 2 * n_in]
    acc = x_ref[...]
    for o_ref, w_ref in zip(o_refs, w_refs):
        acc = acc + _dot(o_ref[...], w_ref[...])
    out_ref[...] = acc


def _out_proj(x, os_, w, layer, tm, tn):
    t, d = x.shape
    n_in = len(os_)
    in_specs = [pl.BlockSpec((tm, tn), lambda i, j: (i, j))]
    in_specs += [pl.BlockSpec((tm, o.shape[1]), lambda i, j: (i, 0)) for o in os_]
    in_specs += [_layer_spec(o.shape[1], tn, layer, lambda i, j, k=k: (k, j)) for k, o in enumerate(os_)]
    ws = [w] * n_in
    return pl.pallas_call(
        functools.partial(_out_proj_kernel, n_in=n_in),
        grid=(t // tm, d // tn),
        in_specs=in_specs,
        out_specs=pl.BlockSpec((tm, tn), lambda i, j: (i, j)),
        out_shape=jax.ShapeDtypeStruct((t, d), F32),
        compiler_params=_params("parallel", "arbitrary"),
        name="out_proj",
    )(x, *os_, *ws)


def _mlp_kernel(x_ref, g_ref, w1_ref, w2_ref, o_ref, xn_ref):
    @pl.when(pl.program_id(1) == 0)
    def _():
        x = x_ref[...]
        xn_ref[...] = _rms_rows(x, g_ref[...]).astype(BF16)
        o_ref[...] = x

    h = jnp.maximum(_dot(xn_ref[...], w1_ref[...]), 0.0)
    o_ref[...] += _dot((h * h).astype(BF16), w2_ref[...])


def _mlp(x, g, w1, w2, layer, tm, tf):
    t, d = x.shape
    f = w1.shape[2]
    return pl.pallas_call(
        _mlp_kernel,
        grid=(t // tm, f // tf),
        in_specs=[
            pl.BlockSpec((tm, d), lambda i, j: (i, 0)),
            pl.BlockSpec((1, d), lambda i, j: (0, 0)),
            _layer_spec(d, tf, layer, lambda i, j: (0, j)),
            _layer_spec(tf, d, layer, lambda i, j: (j, 0)),
        ],
        out_specs=pl.BlockSpec((tm, d), lambda i, j: (i, 0)),
        out_shape=jax.ShapeDtypeStruct((t, d), F32),
        scratch_shapes=[pltpu.VMEM((tm, d), BF16)],
        compiler_params=_params("parallel", "arbitrary"),
        name="mlp",
    )(x, g, w1, w2)


def _rope_tables(seq, dim):
    half = dim // 2
    inv = 1.0 / (ROPE_THETA ** (jnp.arange(0, dim, 2, dtype=F32) / dim))
    ang = jnp.arange(seq, dtype=F32)[:, None] * inv[None, :]
    cos, sin = jnp.cos(ang), jnp.sin(ang)
    reps = LANES // dim
    cos_t = jnp.tile(jnp.concatenate([cos, cos], axis=-1), (1, reps))
    sin_t = jnp.tile(jnp.concatenate([-sin, sin], axis=-1), (1, reps))
    return cos_t, sin_t


def _rotate_half(y, dim):
    half = dim // 2
    if dim == LANES:
        return pltpu.roll(y, half, axis=1)
    lane = lax.broadcasted_iota(jnp.int32, y.shape, 1)
    first = (lane % dim) < half
    return jnp.where(first, pltpu.roll(y, LANES - half, axis=1), pltpu.roll(y, half, axis=1))


def _qk64_kernel(p_ref, g_ref, cos_ref, sin_ref, grp_ref, o_ref):
    cos = cos_ref[...]
    sin = sin_ref[...]
    grp = grp_ref[...]
    for t in range(p_ref.shape[2] // LANES):
        cols = slice(t * LANES, (t + 1) * LANES)
        x = p_ref[0, :, cols].astype(F32)
        hi, lo = _split_bf16(x * x)
        ms = (_dot(hi, grp) + _dot(lo, grp)) * (1.0 / A_SUB)
        y = x * lax.rsqrt(ms + NORM_EPS) * g_ref[:, cols]
        o_ref[0, :, cols] = (y * cos + _rotate_half(y, A_SUB) * sin).astype(o_ref.dtype)


def _qk64_prologue(proj, gain, cos_t, sin_t, width, ts, cw):
    b, s, _ = proj.shape
    lane = jnp.arange(LANES)
    grp = (lane[:, None] // A_SUB == lane[None, :] // A_SUB).astype(BF16)
    return pl.pallas_call(
        _qk64_kernel,
        grid=(b, s // ts, width // cw),
        in_specs=[
            pl.BlockSpec((1, ts, cw), lambda bi, i, j: (bi, i, j)),
            pl.BlockSpec((1, cw), lambda bi, i, j: (0, j)),
            pl.BlockSpec((ts, LANES), lambda bi, i, j: (i, 0)),
            pl.BlockSpec((ts, LANES), lambda bi, i, j: (i, 0)),
            pl.BlockSpec((LANES, LANES), lambda bi, i, j: (0, 0)),
        ],
        out_specs=pl.BlockSpec((1, ts, cw), lambda bi, i, j: (bi, i, j)),
        out_shape=jax.ShapeDtypeStruct((b, s, width), BF16),
        compiler_params=_params("parallel", "parallel", "arbitrary"),
        name="qk64_prologue",
    )(proj, gain, cos_t, sin_t, grp)


DIFF_HEADS_PER_STEP = 2
DEN_ROWS = 16


def _diff_attn_kernel(q_ref, k_ref, v_ref, lam_ref, gn_ref, o_ref, q2_ref, vt_ref, m_ref, acc_ref,
                      s_a, s_b, p_a, p_b, al_a, al_b, *, seq, t, hp, lam_init):
    heads = range(hp)
    qb = 2 * t
    nq = 2 * qb
    cols = [slice(h * HEAD_DIM, (h + 1) * HEAD_DIM) for h in heads]
    lv = lam_ref[...]
    lam = (jnp.exp(jnp.sum(lv[0:1] * lv[1:2], axis=1, keepdims=True))
           - jnp.exp(jnp.sum(lv[2:3] * lv[3:4], axis=1, keepdims=True)) + lam_init)
    key_idx = lax.broadcasted_iota(jnp.int32, (t, LANES), 0)
    lane_idx = lax.broadcasted_iota(jnp.int32, (t, LANES), 1)
    lane = lax.broadcasted_iota(jnp.int32, (qb, HEAD_DIM), 1)

    def key_rows(j):
        return pl.ds(pl.multiple_of(j * t, t), t)

    eye = (lax.broadcasted_iota(jnp.int32, (HEAD_DIM, HEAD_DIM), 0)
           == lax.broadcasted_iota(jnp.int32, (HEAD_DIM, HEAD_DIM), 1)).astype(BF16)

    def transpose_values(j, carry):
        for h in heads:
            vt_ref[h, j, :HEAD_DIM, :] = _dot_nt(eye, v_ref[0, key_rows(j), cols[h]]).astype(BF16)
            vt_ref[h, j, HEAD_DIM:, :] = jnp.ones((DEN_ROWS, t), BF16)
        return carry

    lax.fori_loop(0, seq // t, transpose_values, 0)

    def scores(j, s_ref, h, lanes):
        s_ref[h, :, lanes] = _dot_nt(k_ref[0, key_rows(j), cols[h]], q2_ref[h, lanes, :])

    def values(j, p_ref, al_ref, h, lanes):
        pv = _dot(vt_ref[h, j], p_ref[h, :, lanes])
        acc_ref[h, :, lanes] = al_ref[h, :, lanes] * acc_ref[h, :, lanes] + pv

    def softmax(s_ref, p_ref, al_ref, key_off, h, c):
        strip = slice(c * LANES, (c + 1) * LANES)
        q_lo = (c * LANES) % qb
        if key_off is not None and key_off > q_lo + LANES - 1:
            p_ref[h, :, strip] = jnp.zeros((t, LANES), BF16)
            al_ref[h, :, strip] = jnp.ones((1, LANES), F32)
            return
        s = s_ref[h, :, strip]
        if key_off is not None and key_off + t - 1 > q_lo:
            s = jnp.where(key_idx + key_off <= lane_idx + q_lo, s, -jnp.inf)
        m_prev = m_ref[h, :, strip]
        m_new = jnp.maximum(m_prev, jnp.max(s, axis=0, keepdims=True))
        alpha = jnp.exp2(m_prev - m_new)
        p = jnp.exp2(s - m_new)
        m_ref[h, :, strip] = m_new
        al_ref[h, :, strip] = alpha
        p_ref[h, :, strip] = p.astype(BF16)

    def stage(prev_tile, p_prev, al_prev, next_tile, s_next, s_cur, p_cur, al_cur, key_off):
        for c in range(nq // MXU_COLS):
            lanes = slice(c * MXU_COLS, (c + 1) * MXU_COLS)
            for h in heads:
                values(prev_tile, p_prev, al_prev, h, lanes)
                if next_tile is not None:
                    scores(next_tile, s_next, h, lanes)
                for strip in range(c * MXU_COLS // LANES, (c + 1) * MXU_COLS // LANES):
                    softmax(s_cur, p_cur, al_cur, key_off, h, strip)

    def q_block(i, carry):
        rows = pl.ds(pl.multiple_of(i * qb, qb), qb)
        for h in heads:
            q = q_ref[0, rows, cols[h]]
            zero = jnp.zeros_like(q)
            q2_ref[h, :qb, :] = jnp.where(lane < A_SUB, q, zero)
            q2_ref[h, qb:, :] = jnp.where(lane < A_SUB, zero, q)
        m_ref[...] = jnp.full(m_ref.shape, -jnp.inf, F32)
        acc_ref[...] = jnp.zeros(acc_ref.shape, F32)
        p_b[...] = jnp.zeros(p_b.shape, BF16)
        al_b[...] = jnp.ones(al_b.shape, F32)
        for c in range(nq // MXU_COLS):
            for h in heads:
                scores(0, s_a, h, slice(c * MXU_COLS, (c + 1) * MXU_COLS))

        def visible_pair(jj, c):
            e = 2 * jj
            stage(jnp.maximum(e - 1, 0), p_b, al_b, e + 1, s_b, s_a, p_a, al_a, None)
            stage(e, p_a, al_a, e + 2, s_a, s_b, p_b, al_b, None)
            return c

        lax.fori_loop(0, i, visible_pair, 0)
        e = 2 * i
        stage(jnp.maximum(e - 1, 0), p_b, al_b, e + 1, s_b, s_a, p_a, al_a, 0)
        stage(e, p_a, al_a, None, None, s_b, p_b, al_b, t)
        for c in range(nq // MXU_COLS):
            for h in heads:
                values(e + 1, p_b, al_b, h, slice(c * MXU_COLS, (c + 1) * MXU_COLS))
        for h in heads:
            o12 = acc_ref[h, :HEAD_DIM, :] / acc_ref[h, HEAD_DIM:HEAD_DIM + 1, :]
            o = (o12[:, :qb] - lam * o12[:, qb:]).T
            o_ref[0, rows, cols[h]] = _rms_rows(o, gn_ref[...]).astype(o_ref.dtype)
        return carry

    lax.fori_loop(0, seq // qb, q_block, 0)


def _diff_attention(qk, proj, v_col0, a_lambda, gn, lam_init, n_heads, t):
    b, s, _ = qk.shape
    hp = DIFF_HEADS_PER_STEP
    width = hp * HEAD_DIM
    nq = 4 * t
    kernel = functools.partial(_diff_attn_kernel, seq=s, t=t, hp=hp, lam_init=lam_init)
    k_blk0 = n_heads // hp
    v_blk0 = v_col0 // width
    seq_spec = lambda off: pl.BlockSpec((1, s, width), lambda bi, g: (bi, 0, off + g))
    return pl.pallas_call(
        kernel,
        grid=(b, n_heads // hp),
        in_specs=[
            seq_spec(0), seq_spec(k_blk0), seq_spec(v_blk0),
            pl.BlockSpec((4, A_SUB), lambda bi, g: (0, 0)),
            pl.BlockSpec((1, HEAD_DIM), lambda bi, g: (0, 0)),
        ],
        out_specs=seq_spec(0),
        out_shape=jax.ShapeDtypeStruct((b, s, n_heads * HEAD_DIM), BF16),
        scratch_shapes=[
            pltpu.VMEM((hp, nq, HEAD_DIM), BF16),
            pltpu.VMEM((hp, s // t, HEAD_DIM + DEN_ROWS, t), BF16),
            pltpu.VMEM((hp, 1, nq), F32),
            pltpu.VMEM((hp, HEAD_DIM + DEN_ROWS, nq), F32),
            pltpu.VMEM((hp, t, nq), F32), pltpu.VMEM((hp, t, nq), F32),
            pltpu.VMEM((hp, t, nq), BF16), pltpu.VMEM((hp, t, nq), BF16),
            pltpu.VMEM((hp, 1, nq), F32), pltpu.VMEM((hp, 1, nq), F32),
        ],
        compiler_params=_params("parallel", "parallel"),
        name="diff_attention",
    )(qk, qk, proj, a_lambda, gn)


DILATED_UNROLL = 8


def _dilated_kernel(q_ref, k_ref, v_ref, gq_ref, gk_ref, cos_ref, sin_ref, o_ref,
                    qf_ref, kf_ref, vf_ref, ob0, ob1, ob2, ls0, ls1, ls2, *, seq, rows, unroll):
    def prep(c, carry):
        sl = pl.ds(pl.multiple_of(c * rows, rows), rows)
        cos = cos_ref[sl, :]
        sin = sin_ref[sl, :]
        for src, g_ref, dst in ((q_ref, gq_ref, qf_ref), (k_ref, gk_ref, kf_ref)):
            y = _rms_rows(src[0, sl, :].astype(F32), g_ref[...])
            dst[sl, :] = y * cos + _rotate_half(y, HEAD_DIM) * sin
        vf_ref[sl, :] = v_ref[0, sl, :].astype(F32)
        return carry

    lax.fori_loop(0, seq // rows, prep, 0)

    for (window, dil), ob_ref, ls_ref in zip(C_BRANCHES, (ob0, ob1, ob2), (ls0, ls1, ls2)):
        hops = window // dil
        span = dil * hops
        n_blk = seq // span
        run = min(unroll, n_blk)
        segs = unroll // run
        segs_per_class = n_blk // run
        qi = lax.broadcasted_iota(jnp.int32, (hops, 2 * hops), 0)
        kj = lax.broadcasted_iota(jnp.int32, (hops, 2 * hops), 1)
        band = jnp.logical_and(kj >= qi, kj <= qi + hops)
        own_band = (lax.broadcasted_iota(jnp.int32, (hops, hops), 1)
                    <= lax.broadcasted_iota(jnp.int32, (hops, hops), 0))

        def blocks(it, carry, dil=dil, hops=hops, span=span, run=run, segs=segs,
                   segs_per_class=segs_per_class, ob_ref=ob_ref, ls_ref=ls_ref, band=band,
                   own_band=own_band):
            blks = []
            for g in range(segs):
                seg = it * segs + g
                r = seg // segs_per_class
                n0 = (seg - r * segs_per_class) * run
                base = r + n0 * span
                at = lambda u: pl.ds(base + u * span, hops, stride=dil)
                starts_class = segs_per_class == 1
                if starts_class:
                    kb, vb = [None], [None]
                else:
                    before = pl.ds(jnp.maximum(base - span, r), hops, stride=dil)
                    kb = [kf_ref[before, :]]
                    vb = [vf_ref[before, :]]
                kb += [kf_ref[at(u), :] for u in range(run)]
                vb += [vf_ref[at(u), :] for u in range(run)]
                for u in range(run):
                    q = qf_ref[at(u), :].astype(BF16)
                    if kb[u] is None:
                        blks.append((at(u), q, kb[u + 1].astype(BF16), vb[u + 1].astype(BF16),
                                     own_band))
                    else:
                        mask = band if u > 0 else jnp.logical_and(
                            band, jnp.logical_or(kj >= hops, n0 > 0))
                        blks.append((at(u), q,
                                     jnp.concatenate([kb[u], kb[u + 1]], axis=0).astype(BF16),
                                     jnp.concatenate([vb[u], vb[u + 1]], axis=0).astype(BF16), mask))
            s = [jnp.where(mask, _dot_nt(q, k), -jnp.inf) for _, q, k, _, mask in blks]
            m = [jnp.max(sb, axis=1, keepdims=True) for sb in s]
            p = [jnp.exp2(sb - mb) for sb, mb in zip(s, m)]
            den = [jnp.sum(pb, axis=1, keepdims=True) for pb in p]
            o = [_dot(pb.astype(BF16), blk[3]) for pb, blk in zip(p, blks)]
            for blk, ob, mb, db in zip(blks, o, m, den):
                ob_ref[blk[0], :] = ob / db
                ls_ref[blk[0], :] = jnp.broadcast_to(mb + jnp.log2(db), (hops, HEAD_DIM))
            return carry

        lax.fori_loop(0, dil * n_blk // unroll, blocks, 0)

    def merge(c, carry):
        sl = pl.ds(pl.multiple_of(c * rows, rows), rows)
        l0, l1, l2 = ls0[sl, :], ls1[sl, :], ls2[sl, :]
        m = jnp.maximum(jnp.maximum(l0, l1), l2)
        e0, e1, e2 = jnp.exp2(l0 - m), jnp.exp2(l1 - m), jnp.exp2(l2 - m)
        o = (e0 * ob0[sl, :] + e1 * ob1[sl, :] + e2 * ob2[sl, :]) / (e0 + e1 + e2)
        o_ref[0, sl, :] = o.astype(o_ref.dtype)
        return carry

    lax.fori_loop(0, seq // rows, merge, 0)


def _dilated_attention(proj, gq, gk, cos_t, sin_t, n_heads, rows):
    b, s, _ = proj.shape
    seq_spec = lambda off: pl.BlockSpec((1, s, HEAD_DIM), lambda bi, h: (bi, 0, off + h))
    full = lambda shape: pl.BlockSpec(shape, lambda bi, h: (0, 0))
    return pl.pallas_call(
        functools.partial(_dilated_kernel, seq=s, rows=rows, unroll=DILATED_UNROLL),
        grid=(b, n_heads),
        in_specs=[seq_spec(0), seq_spec(n_heads), seq_spec(2 * n_heads),
                  full((1, HEAD_DIM)), full((1, HEAD_DIM)),
                  full((s, HEAD_DIM)), full((s, HEAD_DIM))],
        out_specs=seq_spec(0),
        out_shape=jax.ShapeDtypeStruct((b, s, n_heads * HEAD_DIM), BF16),
        scratch_shapes=[pltpu.VMEM((s, HEAD_DIM), F32) for _ in range(9)],
        compiler_params=_params("parallel", "parallel"),
        name="dilated_attention",
    )(proj, proj, proj, gq, gk, cos_t, sin_t)


HALO = 16


def _conv_kernel(cur_ref, halo_ref, w_ref, o_ref, *, ts, q_scale):
    i = pl.program_id(1)
    j = pl.program_id(2)
    halo = jnp.where(i > 0, halo_ref[0].astype(F32), 0.0)
    xin = jnp.concatenate([halo, cur_ref[0].astype(F32)], axis=0)
    w = w_ref[...]
    y = None
    for tap in range(CONV_K):
        lo = HALO - (CONV_K - 1) + tap
        term = w[tap:tap + 1, :] * xin[lo:lo + ts, :]
        y = term if y is None else y + term
    y = y * jax.nn.sigmoid(y)
    scale = jnp.where(j == 0, q_scale, 1.0)
    for t in range(y.shape[1] // HEAD_DIM):
        cols = slice(t * HEAD_DIM, (t + 1) * HEAD_DIM)
        yt = y[:, cols]
        unit = yt * (lax.rsqrt(jnp.sum(yt * yt, axis=-1, keepdims=True) + NORM_EPS) * scale)
        o_ref[0, :, cols] = jnp.where(j < 2, unit, yt).astype(o_ref.dtype)


def _gdn_prologue(proj, col0, width, conv_w, ts):
    b, s, _ = proj.shape
    blk0 = col0 // width
    rows_per_halo = ts // HALO
    return pl.pallas_call(
        functools.partial(_conv_kernel, ts=ts, q_scale=HEAD_DIM ** -0.5),
        grid=(b, s // ts, 3),
        in_specs=[
            pl.BlockSpec((1, ts, width), lambda bi, i, j: (bi, i, blk0 + j)),
            pl.BlockSpec((1, HALO, width),
                         lambda bi, i, j: (bi, jnp.maximum(i * rows_per_halo - 1, 0), blk0 + j)),
            pl.BlockSpec((CONV_K, width), lambda bi, i, j: (0, j)),
        ],
        out_specs=pl.BlockSpec((1, ts, width), lambda bi, i, j: (bi, i, j)),
        out_shape=jax.ShapeDtypeStruct((b, s, 3 * width), BF16),
        compiler_params=_params("parallel", "parallel", "arbitrary"),
        name="gdn_prologue",
    )(proj, proj, conv_w)


GDN_CHUNKS_PER_STEP = 2


def _dot_x3(a, b):
    (ah, al), (bh, bl) = a, b
    return _dot(ah, bh) + (_dot(ah, bl) + _dot(al, bh))


def _unit_lower_inverses(ms):
    n = ms[0].shape[0]
    row = lax.broadcasted_iota(jnp.int32, (n, n), 0)
    col = lax.broadcasted_iota(jnp.int32, (n, n), 1)
    eye = jnp.where(row == col, 1.0, 0.0)
    invs = [eye - m for m in ms]
    powers = [_split_bf16(m) for m in ms]
    k = 2
    while k < n:
        powers = [_split_bf16(_dot_x3(p, p)) for p in powers]
        invs = [inv + _dot_x3(_split_bf16(inv), p) for inv, p in zip(invs, powers)]
        k *= 2
    return invs


def _gdn_kernel(q_ref, k_ref, v_ref, z_ref, gate_ref, alog_ref, dtb_ref, gn_ref, o_ref, state_ref,
                *, n_heads, n_chunks):
    c = HEAD_DIM
    heads = range(n_heads)

    @pl.when(pl.program_id(1) == 0)
    def _():
        state_ref[...] = jnp.zeros(state_ref.shape, F32)

    row = lax.broadcasted_iota(jnp.int32, (CHUNK, CHUNK), 0)
    col = lax.broadcasted_iota(jnp.int32, (CHUNK, CHUNK), 1)
    causal = row >= col
    strict = row > col
    tri = jnp.where(causal, 1.0, 0.0).astype(BF16)
    cols = [slice(h * c, (h + 1) * c) for h in heads]

    pairs = [(ci, h) for ci in range(n_chunks) for h in heads]
    rows = [slice(ci * CHUNK, (ci + 1) * CHUNK) for ci in range(n_chunks)]
    gc, gc_t, beta_all = [], [], []
    for ci in range(n_chunks):
        gates = gate_ref[0, rows[ci], :]
        g = -jnp.exp(alog_ref[...]) * jax.nn.softplus(gates + dtb_ref[...])
        beta_all.append(jax.nn.sigmoid(gates))
        g_hi, g_lo = _split_bf16(g)
        gc.append(_dot(tri, g_hi) + _dot(tri, g_lo))
        gc_t.append(jnp.concatenate([gc[ci], jnp.zeros_like(gc[ci])], axis=0).T)
    gcol = [gc[ci][:, h:h + 1] for ci, h in pairs]
    bcol = [beta_all[ci][:, n_heads + h:n_heads + h + 1] for ci, h in pairs]
    decay_incl = [jnp.exp(jnp.where(causal, gcol[x] - gc_t[ci][h:h + 1, :CHUNK], -jnp.inf))
                  for x, (ci, h) in enumerate(pairs)]
    px = range(len(pairs))
    q = [q_ref[0, rows[ci], cols[h]] for ci, h in pairs]
    k = [k_ref[0, rows[ci], cols[h]] for ci, h in pairs]
    kf = [k[x].astype(F32) for x in px]
    kb = [kf[x] * bcol[x] for x in px]
    m = [_dot_nt(kb[x].astype(BF16), k[x]) * jnp.where(strict, decay_incl[x], 0.0) for x in px]
    inv = _unit_lower_inverses(m)
    egc = [jnp.exp(gcol[x]) for x in px]
    rhs = [jnp.concatenate([v_ref[0, rows[ci], cols[h]].astype(F32) * bcol[x], kb[x] * egc[x]], axis=1)
           for x, (ci, h) in enumerate(pairs)]
    sol = [_dot(inv[x].astype(BF16), rhs[x].astype(BF16)) for x in px]
    attn = [(_dot_nt(q[x], k[x]) * decay_incl[x]).astype(BF16) for x in px]
    g_last = [gc[ci][CHUNK - 1:CHUNK, h:h + 1] for ci, h in pairs]
    q_dec = [(q[x].astype(F32) * egc[x]).astype(BF16) for x in px]
    k_dec = [(kf[x] * jnp.exp(g_last[x] - gcol[x])).astype(BF16) for x in px]

    state = [state_ref[h] for h in heads]
    for ci in range(n_chunks):
        xs = [ci * n_heads + h for h in heads]
        state_b = [s.astype(BF16) for s in state]
        v_new = [(sol[x][:, :c] - _dot(sol[x][:, c:].astype(BF16), state_b[h])).astype(BF16)
                 for h, x in enumerate(xs)]
        o = [_dot(q_dec[x], state_b[h]) + _dot(attn[x], v_new[h]) for h, x in enumerate(xs)]
        state = [state[h] * jnp.exp(g_last[x]) + _dot_tn(k_dec[x], v_new[h]) for h, x in enumerate(xs)]
        for h in heads:
            z = z_ref[0, rows[ci], cols[h]].astype(F32)
            o_ref[0, rows[ci], cols[h]] = (_rms_rows(o[h], gn_ref[...])
                                           * (z * jax.nn.sigmoid(z))).astype(o_ref.dtype)
    for h in heads:
        state_ref[h] = state[h]


def _gated_delta(qkv, proj, z_col0, gates, a_log, dt_bias, gn, n_heads):
    b, s, _ = qkv.shape
    width = n_heads * HEAD_DIM
    z_blk = z_col0 // width
    step_rows = GDN_CHUNKS_PER_STEP * CHUNK
    chunk_spec = lambda blk: pl.BlockSpec((1, step_rows, width), lambda bi, ci: (bi, ci, blk))
    row_spec = pl.BlockSpec((1, LANES), lambda bi, ci: (0, 0))
    return pl.pallas_call(
        functools.partial(_gdn_kernel, n_heads=n_heads, n_chunks=GDN_CHUNKS_PER_STEP),
        grid=(b, s // step_rows),
        in_specs=[chunk_spec(0), chunk_spec(1), chunk_spec(2), chunk_spec(z_blk),
                  pl.BlockSpec((1, step_rows, LANES), lambda bi, ci: (bi, ci, 0)),
                  row_spec, row_spec, row_spec],
        out_specs=chunk_spec(0),
        out_shape=jax.ShapeDtypeStruct((b, s, width), BF16),
        scratch_shapes=[pltpu.VMEM((n_heads, HEAD_DIM, HEAD_DIM), F32)],
        compiler_params=_params("parallel", "arbitrary"),
        name="gated_delta_rule",
    )(qkv, qkv, qkv, proj, gates, a_log, dt_bias, gn)


TM_PROJ, TN_PROJ = 1024, 1024
TM_MLP, TF_MLP = 1024, 512
T_DIFF = 256
TS_ELEM = 512


def _lane_row(v):
    return jnp.zeros((1, LANES), F32).at[0, :v.shape[0]].set(v.astype(F32))


def _diff_delta_layer(xf, b, s, layer_idx, i, norm, w_in, a_q_norm, a_k_norm, a_lambda, a_sub_norm,
                      b_conv, b_a_log, b_dt_bias, b_out_norm, w_out, cos_a, sin_a):
    t, d = xf.shape
    a_width = d // 2
    b_width = d - a_width
    a_heads = a_width // HEAD_DIM
    b_heads = b_width // HEAD_DIM
    main = 3 * a_width + 4 * b_width
    w_gate = jnp.zeros((1, d, LANES), BF16).at[0, :, :2 * b_heads].set(w_in[i, :, main:])
    g = norm[None]
    proj = _norm_matmul(xf, g, w_in, i, main, BF16, TM_PROJ, TN_PROJ).reshape(b, s, main)
    gates = _norm_matmul(xf, g, w_gate, 0, LANES, F32, TM_PROJ, LANES).reshape(b, s, LANES)

    lam_init = 0.8 - 0.6 * math.exp(-0.3 * layer_idx)
    qk_gain = jnp.concatenate([jnp.tile(a_q_norm * (A_SUB ** -0.5 * math.log2(math.e)), 2 * a_heads),
                               jnp.tile(a_k_norm, 2 * a_heads)])[None]
    qk = _qk64_prologue(proj, qk_gain, cos_a, sin_a, 2 * a_width, TS_ELEM, 512)
    oa = _diff_attention(qk, proj, 2 * a_width, a_lambda, (a_sub_norm * (1.0 - lam_init))[None],
                         lam_init, a_heads, T_DIFF)

    qkv = _gdn_prologue(proj, 3 * a_width, b_width, b_conv, TS_ELEM)
    ob = _gated_delta(qkv, proj, 3 * a_width + 3 * b_width, gates, _lane_row(b_a_log),
                      _lane_row(b_dt_bias), b_out_norm[None], b_heads)

    return _out_proj(xf, [oa.reshape(t, a_width), ob.reshape(t, b_width)], w_out, i, TM_PROJ, TN_PROJ)


def _dilated_layer(xf, b, s, i, norm, w_in, q_norm, k_norm, w_out, cos_c, sin_c):
    t, d = xf.shape
    width = w_in.shape[2] // 3
    proj = _norm_matmul(xf, norm[None], w_in, i, 3 * width, BF16, TM_PROJ, TN_PROJ)
    q_gain = q_norm * (HEAD_DIM ** -0.5 * math.log2(math.e))
    o = _dilated_attention(proj.reshape(b, s, 3 * width), q_gain[None],
                           k_norm[None], cos_c, sin_c, width // HEAD_DIM, TS_ELEM)
    return _out_proj(xf, [o.reshape(t, width)], w_out, i, TM_PROJ, TN_PROJ)


def kernel(x, ab_norm, ab_w_in, a_q_norm, a_k_norm, a_lambda, a_sub_norm, b_conv, b_a_log, b_dt_bias,
           b_out_norm, ab_w_out, c_norm, c_w_in, c_q_norm, c_k_norm, c_w_out, mlp_norm, mlp_w1, mlp_w2):
    b, s, d = x.shape
    depth = mlp_w1.shape[0]
    cos_a, sin_a = _rope_tables(s, A_SUB)
    cos_c, sin_c = _rope_tables(s, HEAD_DIM)
    xf = x.reshape(b * s, d)
    ab_w_in, ab_w_out, c_w_in, c_w_out, mlp_w1, mlp_w2 = (
        w.astype(BF16) for w in (ab_w_in, ab_w_out, c_w_in, c_w_out, mlp_w1, mlp_w2))
    for l in range(depth):
        i = l // 2
        if l % 2 == 0:
            xf = _diff_delta_layer(xf, b, s, l, i, ab_norm[i], ab_w_in, a_q_norm[i], a_k_norm[i],
                                   a_lambda[i], a_sub_norm[i], b_conv[i], b_a_log[i], b_dt_bias[i],
                                   b_out_norm[i], ab_w_out, cos_a, sin_a)
        else:
            xf = _dilated_layer(xf, b, s, i, c_norm[i], c_w_in, c_q_norm[i], c_k_norm[i], c_w_out,
                                cos_c, sin_c)
        xf = _mlp(xf, mlp_norm[l][None], mlp_w1, mlp_w2, l, TM_MLP, TF_MLP)
    return xf.reshape(b, s, d)
```

```python
import functools
import math

import jax
import jax.numpy as jnp
from jax import lax
from jax.experimental import pallas as pl
from jax.experimental.pallas import tpu as pltpu

F32 = jnp.float32
BF16 = jnp.bfloat16

HEAD_DIM = 128
A_SUB = HEAD_DIM // 2
CONV_K = 4
CHUNK = 64
ROPE_THETA = 10000.0
NORM_EPS = 1e-6
C_BRANCHES = ((128, 1), (512, 4), (2048, 16))
LANES = 128
MXU_COLS = 256
VMEM_LIMIT_BYTES = 56 * 1024 * 1024


def _params(*semantics):
    return pltpu.CompilerParams(dimension_semantics=semantics, vmem_limit_bytes=VMEM_LIMIT_BYTES)


def _rms_rows(x, gain):
    ms = jnp.mean(x * x, axis=-1, keepdims=True)
    return x * lax.rsqrt(ms + NORM_EPS) * gain


def _dot(a, b):
    return jnp.dot(a, b, preferred_element_type=F32)


def _dot_nt(a, b):
    return lax.dot_general(a, b, (((1,), (1,)), ((), ())), preferred_element_type=F32)


def _dot_tn(a, b):
    return lax.dot_general(a, b, (((0,), (0,)), ((), ())), preferred_element_type=F32)


def _split_bf16(x):
    hi = x.astype(BF16)
    lo = (x - hi.astype(F32)).astype(BF16)
    return hi, lo


def _norm_matmul_kernel(x_ref, g_ref, w_ref, o_ref, xn_ref):
    @pl.when(pl.program_id(1) == 0)
    def _():
        xn_ref[...] = _rms_rows(x_ref[...], g_ref[...]).astype(BF16)

    o_ref[...] = _dot(xn_ref[...], w_ref[...]).astype(o_ref.dtype)


def _layer_spec(rows, cols, layer, index_map):
    return pl.BlockSpec((None, rows, cols), lambda i, j: (layer,) + index_map(i, j))


def _norm_matmul(x, g, w, layer, n, out_dtype, tm, tn):
    t, d = x.shape
    tn = min(tn, n)
    return pl.pallas_call(
        _norm_matmul_kernel,
        grid=(t // tm, n // tn),
        in_specs=[
            pl.BlockSpec((tm, d), lambda i, j: (i, 0)),
            pl.BlockSpec((1, d), lambda i, j: (0, 0)),
            _layer_spec(d, tn, layer, lambda i, j: (0, j)),
        ],
        out_specs=pl.BlockSpec((tm, tn), lambda i, j: (i, j)),
        out_shape=jax.ShapeDtypeStruct((t, n), out_dtype),
        scratch_shapes=[pltpu.VMEM((tm, d), BF16)],
        compiler_params=_params("parallel", "arbitrary"),
        name="norm_matmul",
    )(x, g, w)


def _out_proj_kernel(*refs, n_in):
    x_ref = refs[0]
    o_refs = refs[1:1 + n_in]
    w_refs = refs[1 + n_in:1 + 2 * n_in]
    out_ref = refs[1 + 2 * n_in]
    acc = x_ref[...]
    for o_ref, w_ref in zip(o_refs, w_refs):
        acc = acc + _dot(o_ref[...], w_ref[...])
    out_ref[...] = acc


def _out_proj(x, os_, w, layer, tm, tn):
    t, d = x.shape
    n_in = len(os_)
    in_specs = [pl.BlockSpec((tm, tn), lambda i, j: (i, j))]
    in_specs += [pl.BlockSpec((tm, o.shape[1]), lambda i, j: (i, 0)) for o in os_]
    in_specs += [_layer_spec(o.shape[1], tn, layer, lambda i, j, k=k: (k, j)) for k, o in enumerate(os_)]
    ws = [w] * n_in
    return pl.pallas_call(
        functools.partial(_out_proj_kernel, n_in=n_in),
        grid=(t // tm, d // tn),
        in_specs=in_specs,
        out_specs=pl.BlockSpec((tm, tn), lambda i, j: (i, j)),
        out_shape=jax.ShapeDtypeStruct((t, d), F32),
        compiler_params=_params("parallel", "arbitrary"),
        name="out_proj",
    )(x, *os_, *ws)


def _mlp_kernel(x_ref, g_ref, w1_ref, w2_ref, o_ref, xn_ref):
    @pl.when(pl.program_id(1) == 0)
    def _():
        x = x_ref[...]
        xn_ref[...] = _rms_rows(x, g_ref[...]).astype(BF16)
        o_ref[...] = x

    h = jnp.maximum(_dot(xn_ref[...], w1_ref[...]), 0.0)
    o_ref[...] += _dot((h * h).astype(BF16), w2_ref[...])


def _mlp(x, g, w1, w2, layer, tm, tf):
    t, d = x.shape
    f = w1.shape[2]
    return pl.pallas_call(
        _mlp_kernel,
        grid=(t // tm, f // tf),
        in_specs=[
            pl.BlockSpec((tm, d), lambda i, j: (i, 0)),
            pl.BlockSpec((1, d), lambda i, j: (0, 0)),
            _layer_spec(d, tf, layer, lambda i, j: (0, j)),
            _layer_spec(tf, d, layer, lambda i, j: (j, 0)),
        ],
        out_specs=pl.BlockSpec((tm, d), lambda i, j: (i, 0)),
        out_shape=jax.ShapeDtypeStruct((t, d), F32),
        scratch_shapes=[pltpu.VMEM((tm, d), BF16)],
        compiler_params=_params("parallel", "arbitrary"),
        name="mlp",
    )(x, g, w1, w2)


def _rope_tables(seq, dim):
    half = dim // 2
    inv = 1.0 / (ROPE_THETA ** (jnp.arange(0, dim, 2, dtype=F32) / dim))
    ang = jnp.arange(seq, dtype=F32)[:, None] * inv[None, :]
    cos, sin = jnp.cos(ang), jnp.sin(ang)
    reps = LANES // dim
    cos_t = jnp.tile(jnp.concatenate([cos, cos], axis=-1), (1, reps))
    sin_t = jnp.tile(jnp.concatenate([-sin, sin], axis=-1), (1, reps))
    return cos_t, sin_t


def _rotate_half(y, dim):
    half = dim // 2
    if dim == LANES:
        return pltpu.roll(y, half, axis=1)
    lane = lax.broadcasted_iota(jnp.int32, y.shape, 1)
    first = (lane % dim) < half
    return jnp.where(first, pltpu.roll(y, LANES - half, axis=1), pltpu.roll(y, half, axis=1))


def _qk64_kernel(p_ref, g_ref, cos_ref, sin_ref, grp_ref, o_ref):
    cos = cos_ref[...]
    sin = sin_ref[...]
    grp = grp_ref[...]
    for t in range(p_ref.shape[2] // LANES):
        cols = slice(t * LANES, (t + 1) * LANES)
        x = p_ref[0, :, cols].astype(F32)
        hi, lo = _split_bf16(x * x)
        ms = (_dot(hi, grp) + _dot(lo, grp)) * (1.0 / A_SUB)
        y = x * lax.rsqrt(ms + NORM_EPS) * g_ref[:, cols]
        o_ref[0, :, cols] = (y * cos + _rotate_half(y, A_SUB) * sin).astype(o_ref.dtype)


def _qk64_prologue(proj, gain, cos_t, sin_t, width, ts, cw):
    b, s, _ = proj.shape
    lane = jnp.arange(LANES)
    grp = (lane[:, None] // A_SUB == lane[None, :] // A_SUB).astype(BF16)
    return pl.pallas_call(
        _qk64_kernel,
        grid=(b, s // ts, width // cw),
        in_specs=[
            pl.BlockSpec((1, ts, cw), lambda bi, i, j: (bi, i, j)),
            pl.BlockSpec((1, cw), lambda bi, i, j: (0, j)),
            pl.BlockSpec((ts, LANES), lambda bi, i, j: (i, 0)),
            pl.BlockSpec((ts, LANES), lambda bi, i, j: (i, 0)),
            pl.BlockSpec((LANES, LANES), lambda bi, i, j: (0, 0)),
        ],
        out_specs=pl.BlockSpec((1, ts, cw), lambda bi, i, j: (bi, i, j)),
        out_shape=jax.ShapeDtypeStruct((b, s, width), BF16),
        compiler_params=_params("parallel", "parallel", "arbitrary"),
        name="qk64_prologue",
    )(proj, gain, cos_t, sin_t, grp)


DIFF_HEADS_PER_STEP = 2
DEN_ROWS = 16


def _diff_attn_kernel(q_ref, k_ref, v_ref, lam_ref, gn_ref, o_ref, q2_ref, vt_ref, m_ref, acc_ref,
                      s_a, s_b, p_a, p_b, al_a, al_b, *, seq, t, hp, lam_init):
    heads = range(hp)
    qb = 2 * t
    nq = 2 * qb
    cols = [slice(h * HEAD_DIM, (h + 1) * HEAD_DIM) for h in heads]
    lv = lam_ref[...]
    lam = (jnp.exp(jnp.sum(lv[0:1] * lv[1:2], axis=1, keepdims=True))
           - jnp.exp(jnp.sum(lv[2:3] * lv[3:4], axis=1, keepdims=True)) + lam_init)
    key_idx = lax.broadcasted_iota(jnp.int32, (t, LANES), 0)
    lane_idx = lax.broadcasted_iota(jnp.int32, (t, LANES), 1)
    lane = lax.broadcasted_iota(jnp.int32, (qb, HEAD_DIM), 1)

    def key_rows(j):
        return pl.ds(pl.multiple_of(j * t, t), t)

    eye = (lax.broadcasted_iota(jnp.int32, (HEAD_DIM, HEAD_DIM), 0)
           == lax.broadcasted_iota(jnp.int32, (HEAD_DIM, HEAD_DIM), 1)).astype(BF16)

    def transpose_values(j, carry):
        for h in heads:
            vt_ref[h, j, :HEAD_DIM, :] = _dot_nt(eye, v_ref[0, key_rows(j), cols[h]]).astype(BF16)
            vt_ref[h, j, HEAD_DIM:, :] = jnp.ones((DEN_ROWS, t), BF16)
        return carry

    lax.fori_loop(0, seq // t, transpose_values, 0)

    def scores(j, s_ref, h, lanes):
        s_ref[h, :, lanes] = _dot_nt(k_ref[0, key_rows(j), cols[h]], q2_ref[h, lanes, :])

    def values(j, p_ref, al_ref, h, lanes):
        pv = _dot(vt_ref[h, j], p_ref[h, :, lanes])
        acc_ref[h, :, lanes] = al_ref[h, :, lanes] * acc_ref[h, :, lanes] + pv

    def softmax(s_ref, p_ref, al_ref, key_off, h, c):
        strip = slice(c * LANES, (c + 1) * LANES)
        q_lo = (c * LANES) % qb
        if key_off is not None and key_off > q_lo + LANES - 1:
            p_ref[h, :, strip] = jnp.zeros((t, LANES), BF16)
            al_ref[h, :, strip] = jnp.ones((1, LANES), F32)
            return
        s = s_ref[h, :, strip]
        if key_off is not None and key_off + t - 1 > q_lo:
            s = jnp.where(key_idx + key_off <= lane_idx + q_lo, s, -jnp.inf)
        m_prev = m_ref[h, :, strip]
        m_new = jnp.maximum(m_prev, jnp.max(s, axis=0, keepdims=True))
        alpha = jnp.exp2(m_prev - m_new)
        p = jnp.exp2(s - m_new)
        m_ref[h, :, strip] = m_new
        al_ref[h, :, strip] = alpha
        p_ref[h, :, strip] = p.astype(BF16)

    def hidden(key_off, c):
        if key_off is None:
            return False
        last_query = max((strip * LANES) % qb for strip in
                         range(c * MXU_COLS // LANES, (c + 1) * MXU_COLS // LANES)) + LANES - 1
        return key_off > last_query

    def stage(prev_tile, p_prev, al_prev, next_tile, s_next, s_cur, p_cur, al_cur, key_off,
              next_key_off=None):
        for c in range(nq // MXU_COLS):
            lanes = slice(c * MXU_COLS, (c + 1) * MXU_COLS)
            for h in heads:
                values(prev_tile, p_prev, al_prev, h, lanes)
                if next_tile is not None and not hidden(next_key_off, c):
                    scores(next_tile, s_next, h, lanes)
                for strip in range(c * MXU_COLS // LANES, (c + 1) * MXU_COLS // LANES):
                    softmax(s_cur, p_cur, al_cur, key_off, h, strip)

    def q_block(i, carry):
        rows = pl.ds(pl.multiple_of(i * qb, qb), qb)
        for h in heads:
            q = q_ref[0, rows, cols[h]]
            zero = jnp.zeros_like(q)
            q2_ref[h, :qb, :] = jnp.where(lane < A_SUB, q, zero)
            q2_ref[h, qb:, :] = jnp.where(lane < A_SUB, zero, q)
        m_ref[...] = jnp.full(m_ref.shape, -jnp.inf, F32)
        acc_ref[...] = jnp.zeros(acc_ref.shape, F32)
        p_b[...] = jnp.zeros(p_b.shape, BF16)
        al_b[...] = jnp.ones(al_b.shape, F32)
        for c in range(nq // MXU_COLS):
            for h in heads:
                scores(0, s_a, h, slice(c * MXU_COLS, (c + 1) * MXU_COLS))

        def visible_pair(jj, c):
            e = 2 * jj
            stage(jnp.maximum(e - 1, 0), p_b, al_b, e + 1, s_b, s_a, p_a, al_a, None)
            stage(e, p_a, al_a, e + 2, s_a, s_b, p_b, al_b, None)
            return c

        lax.fori_loop(0, i, visible_pair, 0)
        e = 2 * i
        stage(jnp.maximum(e - 1, 0), p_b, al_b, e + 1, s_b, s_a, p_a, al_a, 0, next_key_off=t)
        stage(e, p_a, al_a, None, None, s_b, p_b, al_b, t)
        for c in range(nq // MXU_COLS):
            if hidden(t, c):
                continue
            for h in heads:
                values(e + 1, p_b, al_b, h, slice(c * MXU_COLS, (c + 1) * MXU_COLS))
        for h in heads:
            o12 = acc_ref[h, :HEAD_DIM, :] / acc_ref[h, HEAD_DIM:HEAD_DIM + 1, :]
            o = (o12[:, :qb] - lam * o12[:, qb:]).T
            o_ref[0, rows, cols[h]] = _rms_rows(o, gn_ref[...]).astype(o_ref.dtype)
        return carry

    lax.fori_loop(0, seq // qb, q_block, 0)


def _diff_attention(qk, proj, v_col0, a_lambda, gn, lam_init, n_heads, t):
    b, s, _ = qk.shape
    hp = DIFF_HEADS_PER_STEP
    width = hp * HEAD_DIM
    nq = 4 * t
    kernel = functools.partial(_diff_attn_kernel, seq=s, t=t, hp=hp, lam_init=lam_init)
    k_blk0 = n_heads // hp
    v_blk0 = v_col0 // width
    seq_spec = lambda off: pl.BlockSpec((1, s, width), lambda bi, g: (bi, 0, off + g))
    return pl.pallas_call(
        kernel,
        grid=(b, n_heads // hp),
        in_specs=[
            seq_spec(0), seq_spec(k_blk0), seq_spec(v_blk0),
            pl.BlockSpec((4, A_SUB), lambda bi, g: (0, 0)),
            pl.BlockSpec((1, HEAD_DIM), lambda bi, g: (0, 0)),
        ],
        out_specs=seq_spec(0),
        out_shape=jax.ShapeDtypeStruct((b, s, n_heads * HEAD_DIM), BF16),
        scratch_shapes=[
            pltpu.VMEM((hp, nq, HEAD_DIM), BF16),
            pltpu.VMEM((hp, s // t, HEAD_DIM + DEN_ROWS, t), BF16),
            pltpu.VMEM((hp, 1, nq), F32),
            pltpu.VMEM((hp, HEAD_DIM + DEN_ROWS, nq), F32),
            pltpu.VMEM((hp, t, nq), F32), pltpu.VMEM((hp, t, nq), F32),
            pltpu.VMEM((hp, t, nq), BF16), pltpu.VMEM((hp, t, nq), BF16),
            pltpu.VMEM((hp, 1, nq), F32), pltpu.VMEM((hp, 1, nq), F32),
        ],
        compiler_params=_params("parallel", "parallel"),
        name="diff_attention",
    )(qk, qk, proj, a_lambda, gn)


DILATED_UNROLL = 8


def _dilated_kernel(q_ref, k_ref, v_ref, gq_ref, gk_ref, cos_ref, sin_ref, o_ref,
                    qf_ref, kf_ref, vf_ref, ob0, ob1, ob2, ls0, ls1, ls2, *, seq, rows, unroll):
    def prep(c, carry):
        sl = pl.ds(pl.multiple_of(c * rows, rows), rows)
        cos = cos_ref[sl, :]
        sin = sin_ref[sl, :]
        for src, g_ref, dst in ((q_ref, gq_ref, qf_ref), (k_ref, gk_ref, kf_ref)):
            y = _rms_rows(src[0, sl, :].astype(F32), g_ref[...])
            dst[sl, :] = y * cos + _rotate_half(y, HEAD_DIM) * sin
        vf_ref[sl, :] = v_ref[0, sl, :].astype(F32)
        return carry

    lax.fori_loop(0, seq // rows, prep, 0)

    for (window, dil), ob_ref, ls_ref in zip(C_BRANCHES, (ob0, ob1, ob2), (ls0, ls1, ls2)):
        hops = window // dil
        span = dil * hops
        n_blk = seq // span
        run = min(unroll, n_blk)
        segs = unroll // run
        segs_per_class = n_blk // run
        qi = lax.broadcasted_iota(jnp.int32, (hops, 2 * hops), 0)
        kj = lax.broadcasted_iota(jnp.int32, (hops, 2 * hops), 1)
        band = jnp.logical_and(kj >= qi, kj <= qi + hops)
        own_band = (lax.broadcasted_iota(jnp.int32, (hops, hops), 1)
                    <= lax.broadcasted_iota(jnp.int32, (hops, hops), 0))

        def blocks(it, carry, dil=dil, hops=hops, span=span, run=run, segs=segs,
                   segs_per_class=segs_per_class, ob_ref=ob_ref, ls_ref=ls_ref, band=band,
                   own_band=own_band):
            blks = []
            for g in range(segs):
                seg = it * segs + g
                r = seg // segs_per_class
                n0 = (seg - r * segs_per_class) * run
                base = r + n0 * span
                at = lambda u: pl.ds(base + u * span, hops, stride=dil)
                starts_class = segs_per_class == 1
                if starts_class:
                    kb, vb = [None], [None]
                else:
                    before = pl.ds(jnp.maximum(base - span, r), hops, stride=dil)
                    kb = [kf_ref[before, :]]
                    vb = [vf_ref[before, :]]
                kb += [kf_ref[at(u), :] for u in range(run)]
                vb += [vf_ref[at(u), :] for u in range(run)]
                for u in range(run):
                    q = qf_ref[at(u), :].astype(BF16)
                    if kb[u] is None:
                        blks.append((at(u), q, kb[u + 1].astype(BF16), vb[u + 1].astype(BF16),
                                     own_band))
                    else:
                        mask = band if u > 0 else jnp.logical_and(
                            band, jnp.logical_or(kj >= hops, n0 > 0))
                        blks.append((at(u), q,
                                     jnp.concatenate([kb[u], kb[u + 1]], axis=0).astype(BF16),
                                     jnp.concatenate([vb[u], vb[u + 1]], axis=0).astype(BF16), mask))
            s = [jnp.where(mask, _dot_nt(q, k), -jnp.inf) for _, q, k, _, mask in blks]
            m = [jnp.max(sb, axis=1, keepdims=True) for sb in s]
            p = [jnp.exp2(sb - mb) for sb, mb in zip(s, m)]
            den = [jnp.sum(pb, axis=1, keepdims=True) for pb in p]
            o = [_dot(pb.astype(BF16), blk[3]) for pb, blk in zip(p, blks)]
            for blk, ob, mb, db in zip(blks, o, m, den):
                ob_ref[blk[0], :] = ob / db
                ls_ref[blk[0], :] = jnp.broadcast_to(mb + jnp.log2(db), (hops, HEAD_DIM))
            return carry

        lax.fori_loop(0, dil * n_blk // unroll, blocks, 0)

    def merge(c, carry):
        sl = pl.ds(pl.multiple_of(c * rows, rows), rows)
        l0, l1, l2 = ls0[sl, :], ls1[sl, :], ls2[sl, :]
        m = jnp.maximum(jnp.maximum(l0, l1), l2)
        e0, e1, e2 = jnp.exp2(l0 - m), jnp.exp2(l1 - m), jnp.exp2(l2 - m)
        o = (e0 * ob0[sl, :] + e1 * ob1[sl, :] + e2 * ob2[sl, :]) / (e0 + e1 + e2)
        o_ref[0, sl, :] = o.astype(o_ref.dtype)
        return carry

    lax.fori_loop(0, seq // rows, merge, 0)


def _dilated_attention(proj, gq, gk, cos_t, sin_t, n_heads, rows):
    b, s, _ = proj.shape
    seq_spec = lambda off: pl.BlockSpec((1, s, HEAD_DIM), lambda bi, h: (bi, 0, off + h))
    full = lambda shape: pl.BlockSpec(shape, lambda bi, h: (0, 0))
    return pl.pallas_call(
        functools.partial(_dilated_kernel, seq=s, rows=rows, unroll=DILATED_UNROLL),
        grid=(b, n_heads),
        in_specs=[seq_spec(0), seq_spec(n_heads), seq_spec(2 * n_heads),
                  full((1, HEAD_DIM)), full((1, HEAD_DIM)),
                  full((s, HEAD_DIM)), full((s, HEAD_DIM))],
        out_specs=seq_spec(0),
        out_shape=jax.ShapeDtypeStruct((b, s, n_heads * HEAD_DIM), BF16),
        scratch_shapes=[pltpu.VMEM((s, HEAD_DIM), F32) for _ in range(9)],
        compiler_params=_params("parallel", "parallel"),
        name="dilated_attention",
    )(proj, proj, proj, gq, gk, cos_t, sin_t)


HALO = 16


def _conv_kernel(cur_ref, halo_ref, w_ref, o_ref, *, ts, q_scale):
    i = pl.program_id(1)
    j = pl.program_id(2)
    halo = jnp.where(i > 0, halo_ref[0].astype(F32), 0.0)
    xin = jnp.concatenate([halo, cur_ref[0].astype(F32)], axis=0)
    w = w_ref[...]
    y = None
    for tap in range(CONV_K):
        lo = HALO - (CONV_K - 1) + tap
        term = w[tap:tap + 1, :] * xin[lo:lo + ts, :]
        y = term if y is None else y + term
    y = y * jax.nn.sigmoid(y)
    scale = jnp.where(j == 0, q_scale, 1.0)
    for t in range(y.shape[1] // HEAD_DIM):
        cols = slice(t * HEAD_DIM, (t + 1) * HEAD_DIM)
        yt = y[:, cols]
        unit = yt * (lax.rsqrt(jnp.sum(yt * yt, axis=-1, keepdims=True) + NORM_EPS) * scale)
        o_ref[0, :, cols] = jnp.where(j < 2, unit, yt).astype(o_ref.dtype)


def _gdn_prologue(proj, col0, width, conv_w, ts):
    b, s, _ = proj.shape
    blk0 = col0 // width
    rows_per_halo = ts // HALO
    return pl.pallas_call(
        functools.partial(_conv_kernel, ts=ts, q_scale=HEAD_DIM ** -0.5),
        grid=(b, s // ts, 3),
        in_specs=[
            pl.BlockSpec((1, ts, width), lambda bi, i, j: (bi, i, blk0 + j)),
            pl.BlockSpec((1, HALO, width),
                         lambda bi, i, j: (bi, jnp.maximum(i * rows_per_halo - 1, 0), blk0 + j)),
            pl.BlockSpec((CONV_K, width), lambda bi, i, j: (0, j)),
        ],
        out_specs=pl.BlockSpec((1, ts, width), lambda bi, i, j: (bi, i, j)),
        out_shape=jax.ShapeDtypeStruct((b, s, 3 * width), BF16),
        compiler_params=_params("parallel", "parallel", "arbitrary"),
        name="gdn_prologue",
    )(proj, proj, conv_w)


GDN_CHUNKS_PER_STEP = 4


def _dot_x3(a, b):
    (ah, al), (bh, bl) = a, b
    return _dot(ah, bh) + (_dot(ah, bl) + _dot(al, bh))


def _unit_lower_inverses(ms):
    n = ms[0].shape[0]
    row = lax.broadcasted_iota(jnp.int32, (n, n), 0)
    col = lax.broadcasted_iota(jnp.int32, (n, n), 1)
    eye = jnp.where(row == col, 1.0, 0.0)
    invs = [eye - m for m in ms]
    powers = [_split_bf16(m) for m in ms]
    k = 2
    while k < n:
        powers = [_split_bf16(_dot_x3(p, p)) for p in powers]
        invs = [inv + _dot_x3(_split_bf16(inv), p) for inv, p in zip(invs, powers)]
        k *= 2
    return invs


def _gdn_kernel(q_ref, k_ref, v_ref, z_ref, gate_ref, alog_ref, dtb_ref, gn_ref, o_ref, state_ref,
                *, n_heads, n_chunks):
    c = HEAD_DIM
    heads = range(n_heads)

    @pl.when(pl.program_id(1) == 0)
    def _():
        state_ref[...] = jnp.zeros(state_ref.shape, F32)

    row = lax.broadcasted_iota(jnp.int32, (CHUNK, CHUNK), 0)
    col = lax.broadcasted_iota(jnp.int32, (CHUNK, CHUNK), 1)
    causal = row >= col
    strict = row > col
    tri = jnp.where(causal, 1.0, 0.0).astype(BF16)
    cols = [slice(h * c, (h + 1) * c) for h in heads]

    pairs = [(ci, h) for ci in range(n_chunks) for h in heads]
    rows = [slice(ci * CHUNK, (ci + 1) * CHUNK) for ci in range(n_chunks)]
    gc, gc_t, beta_all = [], [], []
    for ci in range(n_chunks):
        gates = gate_ref[0, rows[ci], :]
        g = -jnp.exp(alog_ref[...]) * jax.nn.softplus(gates + dtb_ref[...])
        beta_all.append(jax.nn.sigmoid(gates))
        g_hi, g_lo = _split_bf16(g)
        gc.append(_dot(tri, g_hi) + _dot(tri, g_lo))
        gc_t.append(jnp.concatenate([gc[ci], jnp.zeros_like(gc[ci])], axis=0).T)
    gcol = [gc[ci][:, h:h + 1] for ci, h in pairs]
    bcol = [beta_all[ci][:, n_heads + h:n_heads + h + 1] for ci, h in pairs]
    decay_incl = [jnp.exp(jnp.where(causal, gcol[x] - gc_t[ci][h:h + 1, :CHUNK], -jnp.inf))
                  for x, (ci, h) in enumerate(pairs)]
    px = range(len(pairs))
    q = [q_ref[0, rows[ci], cols[h]] for ci, h in pairs]
    k = [k_ref[0, rows[ci], cols[h]] for ci, h in pairs]
    kf = [k[x].astype(F32) for x in px]
    kb = [kf[x] * bcol[x] for x in px]
    m = [_dot_nt(kb[x].astype(BF16), k[x]) * jnp.where(strict, decay_incl[x], 0.0) for x in px]
    inv = _unit_lower_inverses(m)
    egc = [jnp.exp(gcol[x]) for x in px]
    rhs = [jnp.concatenate([v_ref[0, rows[ci], cols[h]].astype(F32) * bcol[x], kb[x] * egc[x]], axis=1)
           for x, (ci, h) in enumerate(pairs)]
    sol = [_dot(inv[x].astype(BF16), rhs[x].astype(BF16)) for x in px]
    attn = [(_dot_nt(q[x], k[x]) * decay_incl[x]).astype(BF16) for x in px]
    g_last = [gc[ci][CHUNK - 1:CHUNK, h:h + 1] for ci, h in pairs]
    q_dec = [(q[x].astype(F32) * egc[x]).astype(BF16) for x in px]
    k_dec = [(kf[x] * jnp.exp(g_last[x] - gcol[x])).astype(BF16) for x in px]

    state = [state_ref[h] for h in heads]
    for ci in range(n_chunks):
        xs = [ci * n_heads + h for h in heads]
        state_b = [s.astype(BF16) for s in state]
        v_new = [(sol[x][:, :c] - _dot(sol[x][:, c:].astype(BF16), state_b[h])).astype(BF16)
                 for h, x in enumerate(xs)]
        o = [_dot(q_dec[x], state_b[h]) + _dot(attn[x], v_new[h]) for h, x in enumerate(xs)]
        state = [state[h] * jnp.exp(g_last[x]) + _dot_tn(k_dec[x], v_new[h]) for h, x in enumerate(xs)]
        for h in heads:
            z = z_ref[0, rows[ci], cols[h]].astype(F32)
            o_ref[0, rows[ci], cols[h]] = (_rms_rows(o[h], gn_ref[...])
                                           * (z * jax.nn.sigmoid(z))).astype(o_ref.dtype)
    for h in heads:
        state_ref[h] = state[h]


def _gated_delta(qkv, proj, z_col0, gates, a_log, dt_bias, gn, n_heads):
    b, s, _ = qkv.shape
    width = n_heads * HEAD_DIM
    z_blk = z_col0 // width
    step_rows = GDN_CHUNKS_PER_STEP * CHUNK
    chunk_spec = lambda blk: pl.BlockSpec((1, step_rows, width), lambda bi, ci: (bi, ci, blk))
    row_spec = pl.BlockSpec((1, LANES), lambda bi, ci: (0, 0))
    return pl.pallas_call(
        functools.partial(_gdn_kernel, n_heads=n_heads, n_chunks=GDN_CHUNKS_PER_STEP),
        grid=(b, s // step_rows),
        in_specs=[chunk_spec(0), chunk_spec(1), chunk_spec(2), chunk_spec(z_blk),
                  pl.BlockSpec((1, step_rows, LANES), lambda bi, ci: (bi, ci, 0)),
                  row_spec, row_spec, row_spec],
        out_specs=chunk_spec(0),
        out_shape=jax.ShapeDtypeStruct((b, s, width), BF16),
        scratch_shapes=[pltpu.VMEM((n_heads, HEAD_DIM, HEAD_DIM), F32)],
        compiler_params=_params("parallel", "arbitrary"),
        name="gated_delta_rule",
    )(qkv, qkv, qkv, proj, gates, a_log, dt_bias, gn)


TM_PROJ, TN_PROJ = 1024, 1024
TM_MLP, TF_MLP = 1024, 512
T_DIFF = 256
TS_ELEM = 512


def _lane_row(v):
    return jnp.zeros((1, LANES), F32).at[0, :v.shape[0]].set(v.astype(F32))


def _diff_delta_layer(xf, b, s, layer_idx, i, norm, w_in, a_q_norm, a_k_norm, a_lambda, a_sub_norm,
                      b_conv, b_a_log, b_dt_bias, b_out_norm, w_out, cos_a, sin_a):
    t, d = xf.shape
    a_width = d // 2
    b_width = d - a_width
    a_heads = a_width // HEAD_DIM
    b_heads = b_width // HEAD_DIM
    main = 3 * a_width + 4 * b_width
    w_gate = jnp.zeros((1, d, LANES), BF16).at[0, :, :2 * b_heads].set(w_in[i, :, main:])
    g = norm[None]
    proj = _norm_matmul(xf, g, w_in, i, main, BF16, TM_PROJ, TN_PROJ).reshape(b, s, main)
    gates = _norm_matmul(xf, g, w_gate, 0, LANES, F32, TM_PROJ, LANES).reshape(b, s, LANES)

    lam_init = 0.8 - 0.6 * math.exp(-0.3 * layer_idx)
    qk_gain = jnp.concatenate([jnp.tile(a_q_norm * (A_SUB ** -0.5 * math.log2(math.e)), 2 * a_heads),
                               jnp.tile(a_k_norm, 2 * a_heads)])[None]
    qk = _qk64_prologue(proj, qk_gain, cos_a, sin_a, 2 * a_width, TS_ELEM, 512)
    oa = _diff_attention(qk, proj, 2 * a_width, a_lambda, (a_sub_norm * (1.0 - lam_init))[None],
                         lam_init, a_heads, T_DIFF)

    qkv = _gdn_prologue(proj, 3 * a_width, b_width, b_conv, TS_ELEM)
    ob = _gated_delta(qkv, proj, 3 * a_width + 3 * b_width, gates, _lane_row(b_a_log),
                      _lane_row(b_dt_bias), b_out_norm[None], b_heads)

    return _out_proj(xf, [oa.reshape(t, a_width), ob.reshape(t, b_width)], w_out, i, TM_PROJ, TN_PROJ)


def _dilated_layer(xf, b, s, i, norm, w_in, q_norm, k_norm, w_out, cos_c, sin_c):
    t, d = xf.shape
    width = w_in.shape[2] // 3
    proj = _norm_matmul(xf, norm[None], w_in, i, 3 * width, BF16, TM_PROJ, TN_PROJ)
    q_gain = q_norm * (HEAD_DIM ** -0.5 * math.log2(math.e))
    o = _dilated_attention(proj.reshape(b, s, 3 * width), q_gain[None],
                           k_norm[None], cos_c, sin_c, width // HEAD_DIM, TS_ELEM)
    return _out_proj(xf, [o.reshape(t, width)], w_out, i, TM_PROJ, TN_PROJ)


def kernel(x, ab_norm, ab_w_in, a_q_norm, a_k_norm, a_lambda, a_sub_norm, b_conv, b_a_log, b_dt_bias,
           b_out_norm, ab_w_out, c_norm, c_w_in, c_q_norm, c_k_norm, c_w_out, mlp_norm, mlp_w1, mlp_w2):
    b, s, d = x.shape
    depth = mlp_w1.shape[0]
    cos_a, sin_a = _rope_tables(s, A_SUB)
    cos_c, sin_c = _rope_tables(s, HEAD_DIM)
    xf = x.reshape(b * s, d)
    ab_w_in, ab_w_out, c_w_in, c_w_out, mlp_w1, mlp_w2 = (
        w.astype(BF16) for w in (ab_w_in, ab_w_out, c_w_in, c_w_out, mlp_w1, mlp_w2))
    for l in range(depth):
        i = l // 2
        if l % 2 == 0:
            xf = _diff_delta_layer(xf, b, s, l, i, ab_norm[i], ab_w_in, a_q_norm[i], a_k_norm[i],
                                   a_lambda[i], a_sub_norm[i], b_conv[i], b_a_log[i], b_dt_bias[i],
                                   b_out_norm[i], ab_w_out, cos_a, sin_a)
        else:
            xf = _dilated_layer(xf, b, s, i, c_norm[i], c_w_in, c_q_norm[i], c_k_norm[i], c_w_out,
                                cos_c, sin_c)
        xf = _mlp(xf, mlp_norm[l][None], mlp_w1, mlp_w2, l, TM_MLP, TF_MLP)
    return xf.reshape(b, s, d)
```
